```python
import math
import jax
import jax.numpy as jnp
from jax import lax
import numpy as np

D_MODEL = 1024
BATCH = 16
SEQ = 4096
DEPTH = 4

HEAD_DIM = 128
MIX_WIDTH = D_MODEL
HALF_WIDTH = MIX_WIDTH // 2
RET_HEADS = HALF_WIDTH // HEAD_DIM
RET_WIDTH = RET_HEADS * HEAD_DIM
RET_CHUNK = 128
POOL_WINDOWS = (2, 4, 8, 16)
POOL_WIDTH = HALF_WIDTH
POOL_GROUP = POOL_WIDTH // len(POOL_WINDOWS)
LRU_WIDTH = HALF_WIDTH
LRU_BLOCKS = 8
LRU_C = 8.0
CONV_WIDTH = 4
ATT_HEADS = HALF_WIDTH // HEAD_DIM
ATT_WIDTH = ATT_HEADS * HEAD_DIM
DIL_PATTERNS = ((128, 1), (512, 4), (2048, 16))
ATT_BLOCK = 128
ROPE_THETA = 10000.0
EVEN_IN = 4 * RET_WIDTH + POOL_WIDTH
ODD_IN = 2 * LRU_WIDTH + 3 * ATT_WIDTH
D_FF = -(-8 * D_MODEL // (3 * 256)) * 256
DEEPNORM_ALPHA = (2 * DEPTH) ** 0.25
DEEPNORM_BETA = (8 * DEPTH) ** -0.25
LN_EPS = 1e-5

kernel_name = "retnet_pool_griffin_longnet_hybrid"

F32 = jnp.float32


def layer_norm(x, g, b):
    xf = x.astype(F32)
    mu = jnp.mean(xf, -1, keepdims=True)
    var = jnp.mean(jnp.square(xf - mu), -1, keepdims=True)
    return ((xf - mu) * lax.rsqrt(var + LN_EPS) * g + b).astype(x.dtype)


def rope_tables(positions, dim):
    inv = ROPE_THETA ** (-jnp.arange(0, dim, 2, dtype=F32) / dim)
    ang = positions.astype(F32)[..., None] * inv
    return jnp.cos(ang), jnp.sin(ang)


def apply_rope(t, cos, sin):
    t1, t2 = jnp.split(t.astype(F32), 2, axis=-1)
    c = cos[:, :, None, :]
    s = sin[:, :, None, :]
    return jnp.concatenate([t1 * c - t2 * s, t2 * c + t1 * s], axis=-1).astype(t.dtype)


def retention(q, k, v):
    B, S, H, dh = q.shape
    C = RET_CHUNK
    nc = S // C
    lg = jnp.log1p(-(2.0 ** (-5.0 - jnp.arange(H, dtype=F32))))
    idx = jnp.arange(C, dtype=F32)
    qc = q.astype(F32).reshape(B, nc, C, H, dh)
    kc = (k.astype(F32) * dh ** -0.5).reshape(B, nc, C, H, dh)
    vc = v.astype(F32).reshape(B, nc, C, H, dh)
    rel = idx[:, None] - idx[None, :]
    decay = jnp.where(rel[None] >= 0, jnp.exp(jnp.maximum(rel, 0.0)[None] * lg[:, None, None]), 0.0)
    scores = jnp.einsum('bnihd,bnjhd->bnhij', qc, kc) * decay
    inner = jnp.einsum('bnhij,bnjhe->bnihe', scores, vc)
    k_decay = jnp.exp((C - 1 - idx)[None, :] * lg[:, None])
    kv = jnp.einsum('bnjhd,hj,bnjhe->nbhde', kc, k_decay, vc)
    chunk_decay = jnp.exp(C * lg)[:, None, None]

    def step(state, kv_n):
        return state * chunk_decay + kv_n, state

    _, prev = lax.scan(step, jnp.zeros((B, H, dh, dh), F32), kv)
    q_decay = jnp.exp((idx + 1.0)[None, :] * lg[:, None])
    cross = jnp.einsum('bnihd,nbhde,hi->bnihe', qc, prev, q_decay)
    return (inner + cross).reshape(B, S, H, dh)


def head_group_norm(y, g):
    mu = jnp.mean(y, -1, keepdims=True)
    var = jnp.mean(jnp.square(y - mu), -1, keepdims=True)
    yn = (y - mu) * lax.rsqrt(var + LN_EPS)
    B, S, H, dh = y.shape
    return yn.reshape(B, S, H * dh) * g


def multiscale_pool(p, pool_w, pool_scale):
    B, S, _ = p.shape
    pg = p.astype(F32).reshape(B, S, len(POOL_WINDOWS), POOL_GROUP)
    cs = jnp.cumsum(pg, axis=1)
    pos1 = jnp.arange(1, S + 1)
    outs = []
    for gi, w in enumerate(POOL_WINDOWS):
        c = cs[:, :, gi]
        c_prev = jnp.pad(c, ((0, 0), (w, 0), (0, 0)))[:, :S]
        cnt = jnp.minimum(pos1, w).astype(F32)[None, :, None]
        outs.append((c - c_prev) / cnt - pg[:, :, gi])
    pooled = jnp.stack(outs, axis=2)
    mixed = jnp.einsum('bsgc,gcd->bsgd', pooled, pool_w)
    return mixed.reshape(B, S, POOL_WIDTH) * pool_scale


def causal_depthwise_conv(u, w, b):
    y = lax.conv_general_dilated(
        u, w.astype(u.dtype)[:, None, :], window_strides=(1,),
        padding=[(CONV_WIDTH - 1, 0)], dimension_numbers=('NWC', 'WIO', 'NWC'),
        feature_group_count=u.shape[-1])
    return y + b


def rg_lru(u, w_a, b_a, w_x, b_x, lam):
    B, S, R = u.shape
    uf = u.astype(F32)
    ub = uf.reshape(B, S, LRU_BLOCKS, R // LRU_BLOCKS)
    r = jax.nn.sigmoid(jnp.einsum('bsnc,ncd->bsnd', ub, w_a).reshape(B, S, R) + b_a)
    i = jax.nn.sigmoid(jnp.einsum('bsnc,ncd->bsnd', ub, w_x).reshape(B, S, R) + b_x)
    log_a = -LRU_C * r * jax.nn.softplus(-lam.astype(F32))
    a = jnp.exp(log_a)
    bseq = jnp.sqrt(-jnp.expm1(2.0 * log_a)) * (i * uf)

    def combine(left, right):
        a1, b1 = left
        a2, b2 = right
        return a1 * a2, a2 * b1 + b2

    _, h = lax.associative_scan(combine, (a, bseq), axis=1)
    return h


def banded_window_attn(q, k, v, n_back):
    N, L, H, dh = q.shape
    QB = ATT_BLOCK
    nb = -(-L // QB)
    Lp = nb * QB
    q = jnp.pad(q.astype(F32), ((0, 0), (0, Lp - L), (0, 0), (0, 0)))
    kp = jnp.pad(k.astype(F32), ((0, 0), (QB, Lp - L), (0, 0), (0, 0)))
    vp = jnp.pad(v.astype(F32), ((0, 0), (QB, Lp - L), (0, 0), (0, 0)))
    qb = q.reshape(N, nb, QB, H, dh)
    kb = jnp.concatenate([kp[:, :Lp].reshape(N, nb, QB, H, dh), kp[:, QB:].reshape(N, nb, QB, H, dh)], axis=2)
    vb = jnp.concatenate([vp[:, :Lp].reshape(N, nb, QB, H, dh), vp[:, QB:].reshape(N, nb, QB, H, dh)], axis=2)
    s = jnp.einsum('nbqhd,nbkhd->nbhqk', qb, kb)
    qpos = jnp.arange(QB)[:, None] + QB
    kpos = jnp.arange(2 * QB)[None, :]
    rel = qpos - kpos
    band = (rel >= 0) & (rel <= n_back)
    valid = band[None] & ((jnp.arange(nb)[:, None, None] > 0) | (kpos >= QB)[None])
    s = jnp.where(valid[None, :, None], s, -jnp.inf)
    m = jnp.max(s, axis=-1, keepdims=True)
    p = jnp.exp(s - m)
    den = jnp.sum(p, axis=-1, keepdims=True)
    o = jnp.einsum('nbhqk,nbkhd->nbqhd', p / den, vb).reshape(N, Lp, H, dh)[:, :L]
    lse = (m + jnp.log(den))[..., 0]
    lse = jnp.transpose(lse, (0, 1, 3, 2)).reshape(N, Lp, H)[:, :L]
    return o, lse


def dilated_attention(q, k, v):
    B, S, H, dh = q.shape
    q = q * dh ** -0.5
    outs, lses = [], []
    for window, dil in DIL_PATTERNS:
        Ld = S // dil

        def to_strided(t):
            return t.reshape(B, Ld, dil, H, dh).transpose(0, 2, 1, 3, 4).reshape(B * dil, Ld, H, dh)

        o, lse = banded_window_attn(to_strided(q), to_strided(k), to_strided(v), window // dil)
        outs.append(o.reshape(B, dil, Ld, H, dh).transpose(0, 2, 1, 3, 4).reshape(B, S, H, dh))
        lses.append(lse.reshape(B, dil, Ld, H).transpose(0, 2, 1, 3).reshape(B, S, H))
    wts = jax.nn.softmax(jnp.stack(lses, axis=0), axis=0)
    return jnp.sum(wts[..., None] * jnp.stack(outs, axis=0), axis=0)


def even_mixer(x, cos, sin, w_in, ret_norm_g, pool_w, pool_scale, w_out):
    B, S, _ = x.shape
    z = x @ w_in
    q, k, v, g, p = jnp.split(z, [RET_WIDTH, 2 * RET_WIDTH, 3 * RET_WIDTH, 4 * RET_WIDTH], axis=-1)
    heads = lambda t: t.reshape(B, S, RET_HEADS, HEAD_DIM)
    ret = retention(apply_rope(heads(q), cos, sin), apply_rope(heads(k), cos, sin), heads(v))
    ret = head_group_norm(ret, ret_norm_g) * jax.nn.silu(g.astype(F32))
    pool = multiscale_pool(p, pool_w, pool_scale)
    cat = jnp.concatenate([ret, pool], axis=-1).astype(x.dtype)
    return cat @ w_out


def odd_mixer(x, cos, sin, w_in, conv_w, conv_b, gate_a_w, gate_a_b, gate_x_w, gate_x_b, lru_lambda, w_out):
    B, S, _ = x.shape
    z = x @ w_in
    gate_in, u, q, k, v = jnp.split(
        z, [LRU_WIDTH, 2 * LRU_WIDTH, 2 * LRU_WIDTH + ATT_WIDTH, 2 * LRU_WIDTH + 2 * ATT_WIDTH], axis=-1)
    u = causal_depthwise_conv(u, conv_w, conv_b)
    y_lru = rg_lru(u, gate_a_w, gate_a_b, gate_x_w, gate_x_b, lru_lambda) * jax.nn.gelu(gate_in.astype(F32))
    heads = lambda t: t.reshape(B, S, ATT_HEADS, HEAD_DIM)
    y_att = dilated_attention(apply_rope(heads(q), cos, sin), apply_rope(heads(k), cos, sin), heads(v))
    cat = jnp.concatenate([y_lru, y_att.reshape(B, S, ATT_WIDTH)], axis=-1).astype(x.dtype)
    return cat @ w_out


def swiglu(x, w_in, w_out):
    gate, up = jnp.split(x @ w_in, 2, axis=-1)
    return (jax.nn.silu(gate) * up) @ w_out


def setup_inputs(seed: int = 0) -> dict:
    key = jax.random.key(seed)
    ks = jax.random.split(key, 24)
    D = D_MODEL
    ne = (DEPTH + 1) // 2
    no = DEPTH // 2
    bw = LRU_WIDTH // LRU_BLOCKS

    def nrm(k, shape, scale):
        return jax.random.normal(k, shape, F32) * scale

    lam_u = jax.random.uniform(ks[14], (no, LRU_WIDTH), F32, minval=0.9, maxval=0.999)
    s = lam_u ** (1.0 / LRU_C)
    lru_lambda = jnp.log(s) - jnp.log1p(-s)
    positions = jnp.broadcast_to(jnp.arange(SEQ, dtype=jnp.int32)[None, :], (BATCH, SEQ))
    return {
        "x": nrm(ks[0], (BATCH, SEQ, D), 1.0),
        "positions": positions,
        "ev_w_in": nrm(ks[1], (ne, D, EVEN_IN), D ** -0.5),
        "ev_ret_norm_g": 1.0 + nrm(ks[2], (ne, RET_WIDTH), 0.02),
        "ev_pool_w": nrm(ks[3], (ne, len(POOL_WINDOWS), POOL_GROUP, POOL_GROUP), POOL_GROUP ** -0.5),
        "ev_pool_scale": 1.0 + nrm(ks[4], (ne, POOL_WIDTH), 0.02),
        "ev_w_out": nrm(ks[5], (ne, MIX_WIDTH, D), MIX_WIDTH ** -0.5 * DEEPNORM_BETA),
        "od_w_in": nrm(ks[6], (no, D, ODD_IN), D ** -0.5),
        "od_conv_w": nrm(ks[7], (no, CONV_WIDTH, LRU_WIDTH), CONV_WIDTH ** -0.5),
        "od_conv_b": nrm(ks[8], (no, LRU_WIDTH), 0.01),
        "od_gate_a_w": nrm(ks[9], (no, LRU_BLOCKS, bw, bw), bw ** -0.5),
        "od_gate_a_b": nrm(ks[10], (no, LRU_WIDTH), 0.01),
        "od_gate_x_w": nrm(ks[11], (no, LRU_BLOCKS, bw, bw), bw ** -0.5),
        "od_gate_x_b": nrm(ks[12], (no, LRU_WIDTH), 0.01),
        "od_lru_lambda": lru_lambda,
        "od_w_out": nrm(ks[13], (no, MIX_WIDTH, D), MIX_WIDTH ** -0.5 * DEEPNORM_BETA),
        "ffn_w_in": nrm(ks[15], (DEPTH, D, 2 * D_FF), D ** -0.5),
        "ffn_w_out": nrm(ks[16], (DEPTH, D_FF, D), D_FF ** -0.5 * DEEPNORM_BETA),
        "ln_g": 1.0 + nrm(ks[17], (DEPTH, 2, D), 0.02),
        "ln_b": nrm(ks[18], (DEPTH, 2, D), 0.02),
    }


def reference(x, positions, ev_w_in, ev_ret_norm_g, ev_pool_w, ev_pool_scale, ev_w_out,
              od_w_in, od_conv_w, od_conv_b, od_gate_a_w, od_gate_a_b, od_gate_x_w, od_gate_x_b,
              od_lru_lambda, od_w_out, ffn_w_in, ffn_w_out, ln_g, ln_b):
    cos, sin = rope_tables(positions, HEAD_DIM)
    h = x
    for layer in range(DEPTH):
        j = layer // 2
        if layer % 2 == 0:
            mix = even_mixer(h, cos, sin, ev_w_in[j], ev_ret_norm_g[j], ev_pool_w[j], ev_pool_scale[j], ev_w_out[j])
        else:
            mix = odd_mixer(h, cos, sin, od_w_in[j], od_conv_w[j], od_conv_b[j], od_gate_a_w[j], od_gate_a_b[j],
                            od_gate_x_w[j], od_gate_x_b[j], od_lru_lambda[j], od_w_out[j])
        h = layer_norm(DEEPNORM_ALPHA * h + mix.astype(h.dtype), ln_g[layer, 0], ln_b[layer, 0])
        h = layer_norm(DEEPNORM_ALPHA * h + swiglu(h, ffn_w_in[layer], ffn_w_out[layer]), ln_g[layer, 1], ln_b[layer, 1])
    return h
```

```python
import functools
import math

import jax
import jax.numpy as jnp
from jax import lax
from jax.experimental import pallas as pl
from jax.experimental.pallas import tpu as pltpu

F32 = jnp.float32
BF16 = jnp.bfloat16

D_MODEL = 1024
DEPTH = 4
HEAD_DIM = 128
HALF_WIDTH = 512
N_HEADS = HALF_WIDTH // HEAD_DIM
RET_CHUNK = 128
POOL_WINDOWS = (2, 4, 8, 16)
POOL_GROUP = 128
POOL_HALO = 32
LRU_BLOCKS = 8
LRU_C = 8.0
CONV_WIDTH = 4
CONV_HALO = 8
DIL_PATTERNS = ((128, 1), (512, 4), (2048, 16))
ATT_BLOCK = 128
ATT_TILE = 2048
ROPE_THETA = 10000.0
EVEN_IN = 5 * HALF_WIDTH
ODD_IN = 5 * HALF_WIDTH
D_FF = 2816
DEEPNORM_ALPHA = (2 * DEPTH) ** 0.25
LN_EPS = 1e-5

VMEM_LIMIT = 56 * 1024 * 1024


def _params(*sem):
    return pltpu.CompilerParams(dimension_semantics=sem, vmem_limit_bytes=VMEM_LIMIT)


def _sigmoid(x):
    return 1.0 / (1.0 + jnp.exp(-x))


def _silu(x):
    return x * _sigmoid(x)


def _gelu_tanh(x):
    c = math.sqrt(2.0 / math.pi)
    return x * (0.5 * (1.0 + jnp.tanh(c * (x + 0.044715 * (x * x * x)))))


def _layer_norm(y, g, b):
    mu = jnp.mean(y, axis=-1, keepdims=True)
    d = y - mu
    var = jnp.mean(d * d, axis=-1, keepdims=True)
    return d * lax.rsqrt(var + LN_EPS) * g + b


def _rope(t, cos, sin_signed):
    return t * cos + pltpu.roll(t, HEAD_DIM // 2, axis=1) * sin_signed


def _dot(a, b):
    return jnp.dot(a, b, preferred_element_type=F32)


def _dot_nt(a, b):
    return lax.dot_general(a, b, (((1,), (1,)), ((), ())), preferred_element_type=F32)


def _rope_table_kernel(pos_ref, inv_ref, sign_ref, cos_ref, sin_ref):
    ang = pos_ref[...].astype(F32) * inv_ref[...]
    cos_ref[...] = jnp.cos(ang)
    sin_ref[...] = jnp.sin(ang) * sign_ref[...]


def _rope_tables(positions):
    B, S = positions.shape
    T = B * S
    ts = 2048
    half = HEAD_DIM // 2
    inv = ROPE_THETA ** (-jnp.arange(0, HEAD_DIM, 2, dtype=F32) / HEAD_DIM)
    inv2 = jnp.concatenate([inv, inv])[None, :]
    sign = jnp.concatenate([-jnp.ones((half,), F32), jnp.ones((half,), F32)])[None, :]
    row = pl.BlockSpec((1, HEAD_DIM), lambda i: (0, 0))
    cos, sin = pl.pallas_call(
        _rope_table_kernel,
        grid=(T // ts,),
        in_specs=[pl.BlockSpec((ts, 1), lambda i: (i, 0)), row, row],
        out_specs=[pl.BlockSpec((ts, HEAD_DIM), lambda i: (i, 0))] * 2,
        out_shape=[jax.ShapeDtypeStruct((T, HEAD_DIM), F32)] * 2,
        compiler_params=_params("arbitrary"),
        name="rope_tables",
    )(positions.reshape(T, 1), inv2, sign)
    return cos.reshape(B, S, HEAD_DIM), sin.reshape(B, S, HEAD_DIM)


def _proj_kernel(x_ref, w_ref, z_ref):
    z_ref[...] = _dot(x_ref[...].astype(BF16), w_ref[...])


def _in_proj(x2d, w_bf16):
    T, D = x2d.shape
    N = w_bf16.shape[1]
    tm = 512
    return pl.pallas_call(
        _proj_kernel,
        grid=(T // tm,),
        in_specs=[pl.BlockSpec((tm, D), lambda i: (i, 0)), pl.BlockSpec((D, N), lambda i: (0, 0))],
        out_specs=pl.BlockSpec((tm, N), lambda i: (i, 0)),
        out_shape=jax.ShapeDtypeStruct((T, N), F32),
        compiler_params=_params("arbitrary"),
        name="in_proj",
    )(x2d, w_bf16)


def _even_kernel(z_ref, cos_ref, sin_ref, decay_ref, qdec_ref, kdec_ref, cdec_ref, gn_ref, pw_ref, ps_ref,
                 ret_ref, pool_ref, state_ref, pbuf_ref, s2_ref, s4_ref, s8_ref, *, ts):
    sblk = pl.program_id(1)
    H = POOL_HALO

    @pl.when(sblk == 0)
    def _():
        state_ref[...] = jnp.zeros_like(state_ref)
        pbuf_ref[0:H, :] = jnp.zeros((H, HALF_WIDTH), F32)

    def chunk_body(c, carry):
        r0 = pl.multiple_of(c * RET_CHUNK, RET_CHUNK)
        rows = pl.ds(r0, RET_CHUNK)
        cos = cos_ref[rows, :]
        sin = sin_ref[rows, :]
        for h in range(N_HEADS):
            lanes = slice(h * HEAD_DIM, (h + 1) * HEAD_DIM)
            q = _rope(z_ref[rows, lanes], cos, sin)
            k = _rope(z_ref[rows, pl.ds(HALF_WIDTH + h * HEAD_DIM, HEAD_DIM)], cos, sin) * (HEAD_DIM ** -0.5)
            v = z_ref[rows, pl.ds(2 * HALF_WIDTH + h * HEAD_DIM, HEAD_DIM)]
            gate = z_ref[rows, pl.ds(3 * HALF_WIDTH + h * HEAD_DIM, HEAD_DIM)]
            qb = q.astype(BF16)
            kb = k.astype(BF16)
            vb = v.astype(BF16)
            scores = _dot_nt(qb, kb) * decay_ref[h]
            inner = _dot(scores.astype(BF16), vb)
            state = state_ref[h]
            cross = _dot(qb, state.astype(BF16)) * qdec_ref[h]
            kd = (k * kdec_ref[h]).T.astype(BF16)
            state_ref[h] = state * cdec_ref[h] + _dot(kd, vb)
            y = inner + cross
            mu = jnp.mean(y, axis=-1, keepdims=True)
            d = y - mu
            var = jnp.mean(d * d, axis=-1, keepdims=True)
            yn = d * lax.rsqrt(var + LN_EPS) * gn_ref[:, lanes]
            ret_ref[rows, lanes] = (yn * _silu(gate)).astype(ret_ref.dtype)
        return carry

    lax.fori_loop(0, ts // RET_CHUNK, chunk_body, 0)

    pbuf_ref[H:H + ts, :] = z_ref[:, pl.ds(4 * HALF_WIDTH, HALF_WIDTH)]
    G = POOL_GROUP
    s2_ref[8:H + ts, :] = pbuf_ref[8:H + ts, :] + pbuf_ref[7:H + ts - 1, :]
    s4_ref[16:H + ts, :] = s2_ref[16:H + ts, G:4 * G] + s2_ref[14:H + ts - 2, G:4 * G]
    s8_ref[24:H + ts, :] = s4_ref[24:H + ts, G:3 * G] + s4_ref[20:H + ts - 4, G:3 * G]
    s16 = s8_ref[32:H + ts, G:2 * G] + s8_ref[24:H + ts - 8, G:2 * G]
    wsums = (s2_ref[H:H + ts, 0:G], s4_ref[H:H + ts, 0:G], s8_ref[H:H + ts, 0:G], s16)
    tpos = sblk * ts + lax.broadcasted_iota(jnp.int32, (ts, 1), 0) + 1
    for gi, w in enumerate(POOL_WINDOWS):
        lanes = slice(gi * G, (gi + 1) * G)
        cnt = jnp.minimum(tpos, w).astype(F32)
        pooled = wsums[gi] / cnt - pbuf_ref[H:H + ts, lanes]
        mixed = _dot(pooled.astype(BF16), pw_ref[gi]) * ps_ref[:, lanes]
        pool_ref[:, lanes] = mixed.astype(pool_ref.dtype)
    pbuf_ref[0:H, :] = pbuf_ref[ts:ts + H, :]


def _even_core(z, cos, sin, ret_norm_g, pool_w_bf16, pool_scale):
    B, S, _ = z.shape
    ts = 512
    C = RET_CHUNK
    lg = jnp.log1p(-(2.0 ** (-5.0 - jnp.arange(N_HEADS, dtype=F32))))
    idx = jnp.arange(C, dtype=F32)
    rel = idx[:, None] - idx[None, :]
    decay = jnp.where(rel[None] >= 0, jnp.exp(jnp.maximum(rel, 0.0)[None] * lg[:, None, None]), 0.0)
    k_decay = jnp.exp((C - 1 - idx)[None, :] * lg[:, None])
    q_decay = jnp.exp((idx + 1.0)[None, :] * lg[:, None])
    chunk_decay = jnp.exp(C * lg)
    qdec = jnp.broadcast_to(q_decay[:, :, None], (N_HEADS, C, HEAD_DIM))
    kdec = jnp.broadcast_to(k_decay[:, :, None], (N_HEADS, C, HEAD_DIM))
    cdec = jnp.broadcast_to(chunk_decay[:, None, None], (N_HEADS, 1, HEAD_DIM))

    const3 = lambda shape: pl.BlockSpec(shape, lambda b, i: (0, 0, 0))
    const2 = lambda shape: pl.BlockSpec(shape, lambda b, i: (0, 0))
    seq = lambda w: pl.BlockSpec((None, ts, w), lambda b, i: (b, i, 0))
    return pl.pallas_call(
        functools.partial(_even_kernel, ts=ts),
        grid=(B, S // ts),
        in_specs=[seq(EVEN_IN), seq(HEAD_DIM), seq(HEAD_DIM),
                  const3((N_HEADS, C, C)), const3((N_HEADS, C, HEAD_DIM)), const3((N_HEADS, C, HEAD_DIM)),
                  const3((N_HEADS, 1, HEAD_DIM)), const2((1, HALF_WIDTH)),
                  const3((len(POOL_WINDOWS), POOL_GROUP, POOL_GROUP)), const2((1, HALF_WIDTH))],
        out_specs=[seq(HALF_WIDTH), seq(HALF_WIDTH)],
        out_shape=[jax.ShapeDtypeStruct((B, S, HALF_WIDTH), BF16)] * 2,
        scratch_shapes=[pltpu.VMEM((N_HEADS, HEAD_DIM, HEAD_DIM), F32),
                        pltpu.VMEM((POOL_HALO + ts, HALF_WIDTH), F32),
                        pltpu.VMEM((POOL_HALO + ts, HALF_WIDTH), F32),
                        pltpu.VMEM((POOL_HALO + ts, 3 * POOL_GROUP), F32),
                        pltpu.VMEM((POOL_HALO + ts, 2 * POOL_GROUP), F32)],
        compiler_params=_params("arbitrary", "arbitrary"),
        name="even_core",
    )(z, cos, sin, decay, qdec, kdec, cdec, ret_norm_g[None, :], pool_w_bf16, pool_scale[None, :])


def _lru_kernel(z_ref, cw_ref, cb_ref, wa_ref, ba_ref, wx_ref, bx_ref, lam_ref, y_ref, ubuf_ref, hcar_ref, *, ts):
    sblk = pl.program_id(1)
    H = CONV_HALO

    @pl.when(sblk == 0)
    def _():
        ubuf_ref[0:H, :] = jnp.zeros((H, HALF_WIDTH), F32)
        hcar_ref[...] = jnp.zeros_like(hcar_ref)

    ubuf_ref[H:H + ts, :] = z_ref[:, pl.ds(HALF_WIDTH, HALF_WIDTH)]
    u = cb_ref[...] + cw_ref[CONV_WIDTH - 1:CONV_WIDTH, :] * ubuf_ref[H:H + ts, :]
    for k in range(CONV_WIDTH - 1):
        off = H - (CONV_WIDTH - 1) + k
        u = u + cw_ref[k:k + 1, :] * ubuf_ref[off:off + ts, :]
    ubuf_ref[0:H, :] = ubuf_ref[ts:ts + H, :]

    ub = u.astype(BF16)
    r = _sigmoid(_dot(ub, wa_ref[...]) + ba_ref[...])
    ig = _sigmoid(_dot(ub, wx_ref[...]) + bx_ref[...])
    nl = -lam_ref[...]
    softplus = jnp.maximum(nl, 0.0) + jnp.log1p(jnp.exp(-jnp.abs(nl)))
    log_a = (-LRU_C * r) * softplus
    a = jnp.exp(log_a)
    bseq = jnp.sqrt(jnp.tanh(-log_a) * (1.0 + a * a)) * (ig * u)

    row = lax.broadcasted_iota(jnp.int32, (ts, 1), 0)
    sh = 1
    while sh < ts:
        a_prev = pltpu.roll(a, sh, axis=0)
        b_prev = pltpu.roll(bseq, sh, axis=0)
        m = row >= sh
        bseq = jnp.where(m, a * b_prev + bseq, bseq)
        a = jnp.where(m, a * a_prev, a)
        sh *= 2
    h = bseq + a * hcar_ref[0:1, :]
    hcar_ref[...] = jnp.broadcast_to(h[ts - 1:ts, :], hcar_ref.shape)
    y_ref[...] = (h * _gelu_tanh(z_ref[:, 0:HALF_WIDTH])).astype(y_ref.dtype)


def _block_diag(w):
    n, c, d = w.shape
    eye = jnp.eye(n, dtype=w.dtype)
    return (w[:, :, None, :] * eye[:, None, :, None]).reshape(n * c, n * d)


def _lru_core(z, conv_w, conv_b, gate_a_w, gate_a_b, gate_x_w, gate_x_b, lam):
    B, S, _ = z.shape
    ts = 512
    W = HALF_WIDTH
    wa = _block_diag(gate_a_w).astype(BF16)
    wx = _block_diag(gate_x_w).astype(BF16)
    const = lambda shape: pl.BlockSpec(shape, lambda b, i: (0, 0))
    return pl.pallas_call(
        functools.partial(_lru_kernel, ts=ts),
        grid=(B, S // ts),
        in_specs=[pl.BlockSpec((None, ts, 2 * W), lambda b, i: (b, i, 0)),
                  const((CONV_WIDTH, W)), const((1, W)), const((W, W)), const((1, W)),
                  const((W, W)), const((1, W)), const((1, W))],
        out_specs=pl.BlockSpec((None, ts, W), lambda b, i: (b, i, 0)),
        out_shape=jax.ShapeDtypeStruct((B, S, W), BF16),
        scratch_shapes=[pltpu.VMEM((CONV_HALO + ts, W), F32), pltpu.VMEM((8, W), F32)],
        compiler_params=_params("arbitrary", "arbitrary"),
        name="lru_core",
    )(z, conv_w, conv_b[None, :], wa, gate_a_b[None, :], wx, gate_x_b[None, :], lam[None, :])


def _att_kernel(q_ref, kp_ref, kc_ref, vp_ref, vc_ref, cosp_ref, cosc_ref, sinp_ref, sinc_ref, o_ref,
                qs_ref, ks_ref, vs_ref, op_ref, lp_ref):
    tile = pl.program_id(1)
    TL = ATT_TILE
    QB = ATT_BLOCK
    cos_c = cosc_ref[...]
    sin_c = sinc_ref[...]
    qs_ref[...] = _rope(q_ref[...], cos_c, sin_c) * (HEAD_DIM ** -0.5)
    ks_ref[0:TL, :] = _rope(kp_ref[...], cosp_ref[...], sinp_ref[...])
    ks_ref[TL:2 * TL, :] = _rope(kc_ref[...], cos_c, sin_c)
    vs_ref[0:TL, :] = vp_ref[...]
    vs_ref[TL:2 * TL, :] = vc_ref[...]

    qi = lax.broadcasted_iota(jnp.int32, (QB, 2 * QB), 0)
    kj = lax.broadcasted_iota(jnp.int32, (QB, 2 * QB), 1)
    band = (kj >= qi) & (kj <= qi + QB)

    for p, (window, dil) in enumerate(DIL_PATTERNS):
        assert window // dil == QB
        nblk = TL // QB

        def block_body(n, carry, p=p, dil=dil):
            res = n % dil
            m = n // dil
            q0 = res + QB * dil * m
            k0 = TL + q0 - QB * dil
            qb = qs_ref[pl.ds(q0, QB, stride=dil), :].astype(BF16)
            kb = ks_ref[pl.ds(k0, 2 * QB, stride=dil), :].astype(BF16)
            vb = vs_ref[pl.ds(k0, 2 * QB, stride=dil), :].astype(BF16)
            s = _dot_nt(qb, kb)
            first_key = jnp.where(jnp.logical_and(tile == 0, m == 0), QB, 0)
            s = jnp.where(band & (kj >= first_key), s, -jnp.inf)
            mx = jnp.max(s, axis=-1, keepdims=True)
            e = jnp.exp(s - mx)
            den = jnp.sum(e, axis=-1, keepdims=True)
            o = _dot(e.astype(BF16), vb) / den
            lse = mx + jnp.log(den)
            op_ref[p, pl.ds(q0, QB, stride=dil), :] = o
            lp_ref[p, pl.ds(q0, QB, stride=dil), :] = jnp.broadcast_to(lse, (QB, HEAD_DIM))
            return carry

        lax.fori_loop(0, nblk, block_body, 0)

    l0, l1, l2 = lp_ref[0], lp_ref[1], lp_ref[2]
    mx = jnp.maximum(jnp.maximum(l0, l1), l2)
    e0, e1, e2 = jnp.exp(l0 - mx), jnp.exp(l1 - mx), jnp.exp(l2 - mx)
    out = (e0 * op_ref[0] + e1 * op_ref[1] + e2 * op_ref[2]) / (e0 + e1 + e2)
    o_ref[...] = out.astype(o_ref.dtype)


def _att_core(z, cos, sin):
    B, S, _ = z.shape
    TL = ATT_TILE
    assert S % TL == 0
    qcol = 2 * HALF_WIDTH // HEAD_DIM
    kcol = qcol + N_HEADS
    vcol = kcol + N_HEADS
    cur = lambda col: pl.BlockSpec((None, TL, HEAD_DIM), lambda b, t, h: (b, t, col + h))
    prev = lambda col: pl.BlockSpec((None, TL, HEAD_DIM), lambda b, t, h: (b, jnp.maximum(t - 1, 0), col + h))
    tab_c = pl.BlockSpec((None, TL, HEAD_DIM), lambda b, t, h: (b, t, 0))
    tab_p = pl.BlockSpec((None, TL, HEAD_DIM), lambda b, t, h: (b, jnp.maximum(t - 1, 0), 0))
    return pl.pallas_call(
        _att_kernel,
        grid=(B, S // TL, N_HEADS),
        in_specs=[cur(qcol), prev(kcol), cur(kcol), prev(vcol), cur(vcol), tab_p, tab_c, tab_p, tab_c],
        out_specs=pl.BlockSpec((None, TL, HEAD_DIM), lambda b, t, h: (b, t, h)),
        out_shape=jax.ShapeDtypeStruct((B, S, HALF_WIDTH), BF16),
        scratch_shapes=[pltpu.VMEM((TL, HEAD_DIM), F32), pltpu.VMEM((2 * TL, HEAD_DIM), F32),
                        pltpu.VMEM((2 * TL, HEAD_DIM), F32),
                        pltpu.VMEM((len(DIL_PATTERNS), TL, HEAD_DIM), F32),
                        pltpu.VMEM((len(DIL_PATTERNS), TL, HEAD_DIM), F32)],
        compiler_params=_params("arbitrary", "arbitrary", "arbitrary"),
        name="att_core",
    )(z, z, z, z, z, cos, cos, sin, sin)


def _out_ln_kernel(x_ref, a1_ref, a2_ref, w_ref, g_ref, b_ref, o_ref):
    W = HALF_WIDTH
    mix = _dot(a1_ref[...], w_ref[0:W, :]) + _dot(a2_ref[...], w_ref[W:2 * W, :])
    o_ref[...] = _layer_norm(DEEPNORM_ALPHA * x_ref[...] + mix, g_ref[...], b_ref[...])


def _out_ln(x2d, a1, a2, w_bf16, g, b):
    T, D = x2d.shape
    tm = 512
    rowblk = lambda w: pl.BlockSpec((tm, w), lambda i: (i, 0))
    const = lambda shape: pl.BlockSpec(shape, lambda i: (0, 0))
    return pl.pallas_call(
        _out_ln_kernel,
        grid=(T // tm,),
        in_specs=[rowblk(D), rowblk(HALF_WIDTH), rowblk(HALF_WIDTH), const((D, D)), const((1, D)), const((1, D))],
        out_specs=rowblk(D),
        out_shape=jax.ShapeDtypeStruct((T, D), F32),
        compiler_params=_params("arbitrary"),
        name="out_ln",
    )(x2d, a1, a2, w_bf16, g[None, :], b[None, :])


def _ffn_kernel(x_ref, wg_ref, wu_ref, wo_ref, g_ref, b_ref, o_ref, acc_ref, xb_ref):
    k = pl.program_id(1)

    @pl.when(k == 0)
    def _():
        xb_ref[...] = x_ref[...].astype(BF16)

    xb = xb_ref[...]
    act = _silu(_dot(xb, wg_ref[...])) * _dot(xb, wu_ref[...])
    part = _dot(act.astype(BF16), wo_ref[...])

    @pl.when(k == 0)
    def _():
        acc_ref[...] = part

    @pl.when(k > 0)
    def _():
        acc_ref[...] += part

    @pl.when(k == pl.num_programs(1) - 1)
    def _():
        o_ref[...] = _layer_norm(DEEPNORM_ALPHA * x_ref[...] + acc_ref[...], g_ref[...], b_ref[...])


def _ffn_ln(x2d, w_in_bf16, w_out_bf16, g, b):
    T, D = x2d.shape
    tm = 512
    nk = 2
    tf = D_FF // nk
    return pl.pallas_call(
        _ffn_kernel,
        grid=(T // tm, nk),
        in_specs=[pl.BlockSpec((tm, D), lambda i, k: (i, 0)),
                  pl.BlockSpec((D, tf), lambda i, k: (0, k)),
                  pl.BlockSpec((D, tf), lambda i, k: (0, k + nk)),
                  pl.BlockSpec((tf, D), lambda i, k: (k, 0)),
                  pl.BlockSpec((1, D), lambda i, k: (0, 0)),
                  pl.BlockSpec((1, D), lambda i, k: (0, 0))],
        out_specs=pl.BlockSpec((tm, D), lambda i, k: (i, 0)),
        out_shape=jax.ShapeDtypeStruct((T, D), F32),
        scratch_shapes=[pltpu.VMEM((tm, D), F32), pltpu.VMEM((tm, D), BF16)],
        compiler_params=_params("arbitrary", "arbitrary"),
        name="ffn_ln",
    )(x2d, w_in_bf16, w_in_bf16, w_out_bf16, g[None, :], b[None, :])


def kernel(x, positions, ev_w_in, ev_ret_norm_g, ev_pool_w, ev_pool_scale, ev_w_out, od_w_in, od_conv_w, od_conv_b, od_gate_a_w, od_gate_a_b, od_gate_x_w, od_gate_x_b, od_lru_lambda, od_w_out, ffn_w_in, ffn_w_out, ln_g, ln_b):
    B, S, D = x.shape
    T = B * S
    cos, sin = _rope_tables(positions)
    h = x.reshape(T, D)
    for layer in range(DEPTH):
        j = layer // 2
        if layer % 2 == 0:
            z = _in_proj(h, ev_w_in[j].astype(BF16)).reshape(B, S, EVEN_IN)
            a1, a2 = _even_core(z, cos, sin, ev_ret_norm_g[j], ev_pool_w[j].astype(BF16), ev_pool_scale[j])
            w_out = ev_w_out[j]
        else:
            z = _in_proj(h, od_w_in[j].astype(BF16)).reshape(B, S, ODD_IN)
            a1 = _lru_core(z, od_conv_w[j], od_conv_b[j], od_gate_a_w[j], od_gate_a_b[j],
                           od_gate_x_w[j], od_gate_x_b[j], od_lru_lambda[j])
            a2 = _att_core(z, cos, sin)
            w_out = od_w_out[j]
        h = _out_ln(h, a1.reshape(T, HALF_WIDTH), a2.reshape(T, HALF_WIDTH), w_out.astype(BF16),
                    ln_g[layer, 0], ln_b[layer, 0])
        h = _ffn_ln(h, ffn_w_in[layer].astype(BF16), ffn_w_out[layer].astype(BF16), ln_g[layer, 1], ln_b[layer, 1])
    return h.reshape(B, S, D)
```

```python
import functools
import math

import jax
import jax.numpy as jnp
from jax import lax
from jax.experimental import pallas as pl
from jax.experimental.pallas import tpu as pltpu

F32 = jnp.float32
BF16 = jnp.bfloat16

D_MODEL = 1024
DEPTH = 4
HEAD_DIM = 128
HALF_WIDTH = 512
N_HEADS = HALF_WIDTH // HEAD_DIM
RET_CHUNK = 128
POOL_WINDOWS = (2, 4, 8, 16)
POOL_GROUP = 128
POOL_HALO = 32
LRU_BLOCKS = 8
LRU_C = 8.0
CONV_WIDTH = 4
CONV_HALO = 8
DIL_PATTERNS = ((128, 1), (512, 4), (2048, 16))
ATT_BLOCK = 128
ATT_TILE = 2048
ROPE_THETA = 10000.0
EVEN_IN = 5 * HALF_WIDTH
ODD_IN = 5 * HALF_WIDTH
D_FF = 2816
DEEPNORM_ALPHA = (2 * DEPTH) ** 0.25
LN_EPS = 1e-5

VMEM_LIMIT = 56 * 1024 * 1024


def _params(*sem):
    return pltpu.CompilerParams(dimension_semantics=sem, vmem_limit_bytes=VMEM_LIMIT)


def _sigmoid(x):
    return 1.0 / (1.0 + jnp.exp(-x))


def _silu(x):
    return x * _sigmoid(x)


def _gelu_tanh(x):
    c = math.sqrt(2.0 / math.pi)
    return x * (0.5 * (1.0 + jnp.tanh(c * (x + 0.044715 * (x * x * x)))))


def _layer_norm(y, g, b):
    mu = jnp.mean(y, axis=-1, keepdims=True)
    d = y - mu
    var = jnp.mean(d * d, axis=-1, keepdims=True)
    return d * lax.rsqrt(var + LN_EPS) * g + b


def _rope(t, cos, sin_signed):
    return t * cos + pltpu.roll(t, HEAD_DIM // 2, axis=1) * sin_signed


def _dot(a, b):
    return jnp.dot(a, b, preferred_element_type=F32)


def _dot_nt(a, b):
    return lax.dot_general(a, b, (((1,), (1,)), ((), ())), preferred_element_type=F32)


def _rope_table_kernel(pos_ref, inv_ref, sign_ref, cos_ref, sin_ref):
    ang = pos_ref[...].astype(F32) * inv_ref[...]
    cos_ref[...] = jnp.cos(ang)
    sin_ref[...] = jnp.sin(ang) * sign_ref[...]


def _rope_tables(positions):
    B, S = positions.shape
    T = B * S
    ts = 2048
    half = HEAD_DIM // 2
    inv = ROPE_THETA ** (-jnp.arange(0, HEAD_DIM, 2, dtype=F32) / HEAD_DIM)
    inv2 = jnp.concatenate([inv, inv])[None, :]
    sign = jnp.concatenate([-jnp.ones((half,), F32), jnp.ones((half,), F32)])[None, :]
    row = pl.BlockSpec((1, HEAD_DIM), lambda i: (0, 0))
    cos, sin = pl.pallas_call(
        _rope_table_kernel,
        grid=(T // ts,),
        in_specs=[pl.BlockSpec((ts, 1), lambda i: (i, 0)), row, row],
        out_specs=[pl.BlockSpec((ts, HEAD_DIM), lambda i: (i, 0))] * 2,
        out_shape=[jax.ShapeDtypeStruct((T, HEAD_DIM), F32)] * 2,
        compiler_params=_params("arbitrary"),
        name="rope_tables",
    )(positions.reshape(T, 1), inv2, sign)
    return cos.reshape(B, S, HEAD_DIM), sin.reshape(B, S, HEAD_DIM)


def _proj_kernel(x_ref, w_ref, z_ref):
    z_ref[...] = _dot(x_ref[...].astype(BF16), w_ref[...])


def _in_proj(x2d, w_bf16):
    T, D = x2d.shape
    N = w_bf16.shape[1]
    tm = 512
    return pl.pallas_call(
        _proj_kernel,
        grid=(T // tm,),
        in_specs=[pl.BlockSpec((tm, D), lambda i: (i, 0)), pl.BlockSpec((D, N), lambda i: (0, 0))],
        out_specs=pl.BlockSpec((tm, N), lambda i: (i, 0)),
        out_shape=jax.ShapeDtypeStruct((T, N), F32),
        compiler_params=_params("arbitrary"),
        name="in_proj",
    )(x2d, w_bf16)


def _even_kernel(z_ref, cos_ref, sin_ref, decay_ref, qdec_ref, kdec_ref, cdec_ref, gn_ref, pw_ref, ps_ref,
                 ret_ref, pool_ref, state_ref, pbuf_ref, s2_ref, s4_ref, s8_ref, *, ts):
    sblk = pl.program_id(1)
    H = POOL_HALO

    @pl.when(sblk == 0)
    def _():
        state_ref[...] = jnp.zeros_like(state_ref)
        pbuf_ref[0:H, :] = jnp.zeros((H, HALF_WIDTH), F32)

    def chunk_body(c, carry):
        r0 = pl.multiple_of(c * RET_CHUNK, RET_CHUNK)
        rows = pl.ds(r0, RET_CHUNK)
        cos = cos_ref[rows, :]
        sin = sin_ref[rows, :]
        for h in range(N_HEADS):
            lanes = slice(h * HEAD_DIM, (h + 1) * HEAD_DIM)
            q = _rope(z_ref[rows, lanes], cos, sin)
            k = _rope(z_ref[rows, pl.ds(HALF_WIDTH + h * HEAD_DIM, HEAD_DIM)], cos, sin) * (HEAD_DIM ** -0.5)
            v = z_ref[rows, pl.ds(2 * HALF_WIDTH + h * HEAD_DIM, HEAD_DIM)]
            gate = z_ref[rows, pl.ds(3 * HALF_WIDTH + h * HEAD_DIM, HEAD_DIM)]
            qb = q.astype(BF16)
            kb = k.astype(BF16)
            vb = v.astype(BF16)
            scores = _dot_nt(qb, kb) * decay_ref[h]
            inner = _dot(scores.astype(BF16), vb)
            state = state_ref[h]
            cross = _dot(qb, state.astype(BF16)) * qdec_ref[h]
            kd = (k * kdec_ref[h]).T.astype(BF16)
            state_ref[h] = state * cdec_ref[h] + _dot(kd, vb)
            y = inner + cross
            mu = jnp.mean(y, axis=-1, keepdims=True)
            d = y - mu
            var = jnp.mean(d * d, axis=-1, keepdims=True)
            yn = d * lax.rsqrt(var + LN_EPS) * gn_ref[:, lanes]
            ret_ref[rows, lanes] = (yn * _silu(gate)).astype(ret_ref.dtype)
        return carry

    lax.fori_loop(0, ts // RET_CHUNK, chunk_body, 0)

    pbuf_ref[H:H + ts, :] = z_ref[:, pl.ds(4 * HALF_WIDTH, HALF_WIDTH)]
    G = POOL_GROUP
    s2_ref[8:H + ts, :] = pbuf_ref[8:H + ts, :] + pbuf_ref[7:H + ts - 1, :]
    s4_ref[16:H + ts, :] = s2_ref[16:H + ts, G:4 * G] + s2_ref[14:H + ts - 2, G:4 * G]
    s8_ref[24:H + ts, :] = s4_ref[24:H + ts, G:3 * G] + s4_ref[20:H + ts - 4, G:3 * G]
    s16 = s8_ref[32:H + ts, G:2 * G] + s8_ref[24:H + ts - 8, G:2 * G]
    wsums = (s2_ref[H:H + ts, 0:G], s4_ref[H:H + ts, 0:G], s8_ref[H:H + ts, 0:G], s16)
    tpos = sblk * ts + lax.broadcasted_iota(jnp.int32, (ts, 1), 0) + 1
    for gi, w in enumerate(POOL_WINDOWS):
        lanes = slice(gi * G, (gi + 1) * G)
        cnt = jnp.minimum(tpos, w).astype(F32)
        pooled = wsums[gi] / cnt - pbuf_ref[H:H + ts, lanes]
        mixed = _dot(pooled.astype(BF16), pw_ref[gi]) * ps_ref[:, lanes]
        pool_ref[:, lanes] = mixed.astype(pool_ref.dtype)
    pbuf_ref[0:H, :] = pbuf_ref[ts:ts + H, :]


def _even_core(z, cos, sin, ret_norm_g, pool_w_bf16, pool_scale):
    B, S, _ = z.shape
    ts = 512
    C = RET_CHUNK
    lg = jnp.log1p(-(2.0 ** (-5.0 - jnp.arange(N_HEADS, dtype=F32))))
    idx = jnp.arange(C, dtype=F32)
    rel = idx[:, None] - idx[None, :]
    decay = jnp.where(rel[None] >= 0, jnp.exp(jnp.maximum(rel, 0.0)[None] * lg[:, None, None]), 0.0)
    k_decay = jnp.exp((C - 1 - idx)[None, :] * lg[:, None])
    q_decay = jnp.exp((idx + 1.0)[None, :] * lg[:, None])
    chunk_decay = jnp.exp(C * lg)
    qdec = jnp.broadcast_to(q_decay[:, :, None], (N_HEADS, C, HEAD_DIM))
    kdec = jnp.broadcast_to(k_decay[:, :, None], (N_HEADS, C, HEAD_DIM))
    cdec = jnp.broadcast_to(chunk_decay[:, None, None], (N_HEADS, 1, HEAD_DIM))

    const3 = lambda shape: pl.BlockSpec(shape, lambda b, i: (0, 0, 0))
    const2 = lambda shape: pl.BlockSpec(shape, lambda b, i: (0, 0))
    seq = lambda w: pl.BlockSpec((None, ts, w), lambda b, i: (b, i, 0))
    return pl.pallas_call(
        functools.partial(_even_kernel, ts=ts),
        grid=(B, S // ts),
        in_specs=[seq(EVEN_IN), seq(HEAD_DIM), seq(HEAD_DIM),
                  const3((N_HEADS, C, C)), const3((N_HEADS, C, HEAD_DIM)), const3((N_HEADS, C, HEAD_DIM)),
                  const3((N_HEADS, 1, HEAD_DIM)), const2((1, HALF_WIDTH)),
                  const3((len(POOL_WINDOWS), POOL_GROUP, POOL_GROUP)), const2((1, HALF_WIDTH))],
        out_specs=[seq(HALF_WIDTH), seq(HALF_WIDTH)],
        out_shape=[jax.ShapeDtypeStruct((B, S, HALF_WIDTH), BF16)] * 2,
        scratch_shapes=[pltpu.VMEM((N_HEADS, HEAD_DIM, HEAD_DIM), F32),
                        pltpu.VMEM((POOL_HALO + ts, HALF_WIDTH), F32),
                        pltpu.VMEM((POOL_HALO + ts, HALF_WIDTH), F32),
                        pltpu.VMEM((POOL_HALO + ts, 3 * POOL_GROUP), F32),
                        pltpu.VMEM((POOL_HALO + ts, 2 * POOL_GROUP), F32)],
        compiler_params=_params("arbitrary", "arbitrary"),
        name="even_core",
    )(z, cos, sin, decay, qdec, kdec, cdec, ret_norm_g[None, :], pool_w_bf16, pool_scale[None, :])


def _lru_kernel(z_ref, cw_ref, cb_ref, wa_ref, ba_ref, wx_ref, bx_ref, lam_ref, y_ref, ubuf_ref, hcar_ref, *, ts):
    sblk = pl.program_id(1)
    H = CONV_HALO

    @pl.when(sblk == 0)
    def _():
        ubuf_ref[0:H, :] = jnp.zeros((H, HALF_WIDTH), F32)
        hcar_ref[...] = jnp.zeros_like(hcar_ref)

    ubuf_ref[H:H + ts, :] = z_ref[:, pl.ds(HALF_WIDTH, HALF_WIDTH)]
    u = cb_ref[...] + cw_ref[CONV_WIDTH - 1:CONV_WIDTH, :] * ubuf_ref[H:H + ts, :]
    for k in range(CONV_WIDTH - 1):
        off = H - (CONV_WIDTH - 1) + k
        u = u + cw_ref[k:k + 1, :] * ubuf_ref[off:off + ts, :]
    ubuf_ref[0:H, :] = ubuf_ref[ts:ts + H, :]

    ub = u.astype(BF16)
    r = _sigmoid(_dot(ub, wa_ref[...]) + ba_ref[...])
    ig = _sigmoid(_dot(ub, wx_ref[...]) + bx_ref[...])
    nl = -lam_ref[...]
    softplus = jnp.maximum(nl, 0.0) + jnp.log1p(jnp.exp(-jnp.abs(nl)))
    log_a = (-LRU_C * r) * softplus
    a = jnp.exp(log_a)
    bseq = jnp.sqrt(jnp.tanh(-log_a) * (1.0 + a * a)) * (ig * u)

    row = lax.broadcasted_iota(jnp.int32, (ts, 1), 0)
    sh = 1
    while sh < ts:
        a_prev = pltpu.roll(a, sh, axis=0)
        b_prev = pltpu.roll(bseq, sh, axis=0)
        m = row >= sh
        bseq = jnp.where(m, a * b_prev + bseq, bseq)
        a = jnp.where(m, a * a_prev, a)
        sh *= 2
    h = bseq + a * hcar_ref[0:1, :]
    hcar_ref[...] = jnp.broadcast_to(h[ts - 1:ts, :], hcar_ref.shape)
    y_ref[...] = (h * _gelu_tanh(z_ref[:, 0:HALF_WIDTH])).astype(y_ref.dtype)


def _block_diag(w):
    n, c, d = w.shape
    eye = jnp.eye(n, dtype=w.dtype)
    return (w[:, :, None, :] * eye[:, None, :, None]).reshape(n * c, n * d)


def _lru_core(z, conv_w, conv_b, gate_a_w, gate_a_b, gate_x_w, gate_x_b, lam):
    B, S, _ = z.shape
    ts = 512
    W = HALF_WIDTH
    wa = _block_diag(gate_a_w).astype(BF16)
    wx = _block_diag(gate_x_w).astype(BF16)
    const = lambda shape: pl.BlockSpec(shape, lambda b, i: (0, 0))
    return pl.pallas_call(
        functools.partial(_lru_kernel, ts=ts),
        grid=(B, S // ts),
        in_specs=[pl.BlockSpec((None, ts, 2 * W), lambda b, i: (b, i, 0)),
                  const((CONV_WIDTH, W)), const((1, W)), const((W, W)), const((1, W)),
                  const((W, W)), const((1, W)), const((1, W))],
        out_specs=pl.BlockSpec((None, ts, W), lambda b, i: (b, i, 0)),
        out_shape=jax.ShapeDtypeStruct((B, S, W), BF16),
        scratch_shapes=[pltpu.VMEM((CONV_HALO + ts, W), F32), pltpu.VMEM((8, W), F32)],
        compiler_params=_params("arbitrary", "arbitrary"),
        name="lru_core",
    )(z, conv_w, conv_b[None, :], wa, gate_a_b[None, :], wx, gate_x_b[None, :], lam[None, :])


def _att_kernel(q_ref, k_ref, v_ref, cos_ref, sin_ref, o_ref, qs_ref, ks_ref, vs_ref, op_ref, lp_ref):
    tile = pl.program_id(2)
    TL = ATT_TILE
    QB = ATT_BLOCK

    @pl.when(tile == 0)
    def _():
        ks_ref[0:TL, :] = jnp.zeros((TL, HEAD_DIM), F32)
        vs_ref[0:TL, :] = jnp.zeros((TL, HEAD_DIM), F32)

    @pl.when(tile > 0)
    def _():
        ks_ref[0:TL, :] = ks_ref[TL:2 * TL, :]
        vs_ref[0:TL, :] = vs_ref[TL:2 * TL, :]

    cos = cos_ref[...]
    sin = sin_ref[...]
    qs_ref[...] = _rope(q_ref[...], cos, sin) * (HEAD_DIM ** -0.5)
    ks_ref[TL:2 * TL, :] = _rope(k_ref[...], cos, sin)
    vs_ref[TL:2 * TL, :] = v_ref[...]

    qi = lax.broadcasted_iota(jnp.int32, (QB, 2 * QB), 0)
    kj = lax.broadcasted_iota(jnp.int32, (QB, 2 * QB), 1)
    band_bias = jnp.where((kj >= qi) & (kj <= qi + QB), 0.0, -jnp.inf)
    first_key = jnp.where(tile == 0, QB, 0)

    for p, (window, dil) in enumerate(DIL_PATTERNS):
        assert window // dil == QB
        for n in range(TL // QB):
            res, m = n % dil, n // dil
            q0 = res + QB * dil * m
            k0 = TL + q0 - QB * dil
            qb = qs_ref[pl.ds(q0, QB, stride=dil), :].astype(BF16)
            kb = ks_ref[pl.ds(k0, 2 * QB, stride=dil), :].astype(BF16)
            vb = vs_ref[pl.ds(k0, 2 * QB, stride=dil), :].astype(BF16)
            s = _dot_nt(qb, kb) + band_bias
            if m == 0:
                s = jnp.where(kj >= first_key, s, -jnp.inf)
            mx = jnp.max(s, axis=-1, keepdims=True)
            e = jnp.exp(s - mx)
            den = jnp.sum(e, axis=-1, keepdims=True)
            o = _dot(e.astype(BF16), vb) / den
            lse = mx + jnp.log(den)
            op_ref[p, pl.ds(q0, QB, stride=dil), :] = o
            lp_ref[p, pl.ds(q0, QB, stride=dil), :] = jnp.broadcast_to(lse, (QB, HEAD_DIM))

    l0, l1, l2 = lp_ref[0], lp_ref[1], lp_ref[2]
    mx = jnp.maximum(jnp.maximum(l0, l1), l2)
    e0, e1, e2 = jnp.exp(l0 - mx), jnp.exp(l1 - mx), jnp.exp(l2 - mx)
    out = (e0 * op_ref[0] + e1 * op_ref[1] + e2 * op_ref[2]) / (e0 + e1 + e2)
    o_ref[...] = out.astype(o_ref.dtype)


def _att_core(z, cos, sin):
    B, S, _ = z.shape
    TL = ATT_TILE
    assert S % TL == 0
    qcol = 2 * HALF_WIDTH // HEAD_DIM
    kcol = qcol + N_HEADS
    vcol = kcol + N_HEADS
    zblk = lambda col: pl.BlockSpec((None, TL, HEAD_DIM), lambda b, h, t: (b, t, col + h))
    tab = pl.BlockSpec((None, TL, HEAD_DIM), lambda b, h, t: (b, t, 0))
    return pl.pallas_call(
        _att_kernel,
        grid=(B, N_HEADS, S // TL),
        in_specs=[zblk(qcol), zblk(kcol), zblk(vcol), tab, tab],
        out_specs=pl.BlockSpec((None, TL, HEAD_DIM), lambda b, h, t: (b, t, h)),
        out_shape=jax.ShapeDtypeStruct((B, S, HALF_WIDTH), BF16),
        scratch_shapes=[pltpu.VMEM((TL, HEAD_DIM), F32), pltpu.VMEM((2 * TL, HEAD_DIM), F32),
                        pltpu.VMEM((2 * TL, HEAD_DIM), F32),
                        pltpu.VMEM((len(DIL_PATTERNS), TL, HEAD_DIM), F32),
                        pltpu.VMEM((len(DIL_PATTERNS), TL, HEAD_DIM), F32)],
        compiler_params=_params("arbitrary", "arbitrary", "arbitrary"),
        name="att_core",
    )(z, z, z, cos, sin)


def _out_ln_kernel(x_ref, a1_ref, a2_ref, w_ref, g_ref, b_ref, o_ref):
    W = HALF_WIDTH
    mix = _dot(a1_ref[...], w_ref[0:W, :]) + _dot(a2_ref[...], w_ref[W:2 * W, :])
    o_ref[...] = _layer_norm(DEEPNORM_ALPHA * x_ref[...] + mix, g_ref[...], b_ref[...])


def _out_ln(x2d, a1, a2, w_bf16, g, b):
    T, D = x2d.shape
    tm = 512
    rowblk = lambda w: pl.BlockSpec((tm, w), lambda i: (i, 0))
    const = lambda shape: pl.BlockSpec(shape, lambda i: (0, 0))
    return pl.pallas_call(
        _out_ln_kernel,
        grid=(T // tm,),
        in_specs=[rowblk(D), rowblk(HALF_WIDTH), rowblk(HALF_WIDTH), const((D, D)), const((1, D)), const((1, D))],
        out_specs=rowblk(D),
        out_shape=jax.ShapeDtypeStruct((T, D), F32),
        compiler_params=_params("arbitrary"),
        name="out_ln",
    )(x2d, a1, a2, w_bf16, g[None, :], b[None, :])


FFN_SUB = 512
FFN_CHUNK = 256


def _ffn_kernel(x_ref, wi_ref, wo_ref, g_ref, b_ref, o_ref, *, tm):
    for r0 in range(0, tm, FFN_SUB):
        rows = slice(r0, r0 + FFN_SUB)
        xb = x_ref[rows, :].astype(BF16)
        acc = None
        for c0 in range(0, D_FF, FFN_CHUNK):
            gate = _dot(xb, wi_ref[:, c0:c0 + FFN_CHUNK])
            up = _dot(xb, wi_ref[:, D_FF + c0:D_FF + c0 + FFN_CHUNK])
            part = _dot((_silu(gate) * up).astype(BF16), wo_ref[c0:c0 + FFN_CHUNK, :])
            acc = part if acc is None else acc + part
        o_ref[rows, :] = _layer_norm(DEEPNORM_ALPHA * x_ref[rows, :] + acc, g_ref[...], b_ref[...])


def _ffn_ln(x2d, w_in_bf16, w_out_bf16, g, b):
    T, D = x2d.shape
    tm = 1024
    resident = lambda shape: pl.BlockSpec(shape, lambda i: (0, 0), pipeline_mode=pl.Buffered(1))
    return pl.pallas_call(
        functools.partial(_ffn_kernel, tm=tm),
        grid=(T // tm,),
        in_specs=[pl.BlockSpec((tm, D), lambda i: (i, 0)),
                  resident((D, 2 * D_FF)), resident((D_FF, D)), resident((1, D)), resident((1, D))],
        out_specs=pl.BlockSpec((tm, D), lambda i: (i, 0)),
        out_shape=jax.ShapeDtypeStruct((T, D), F32),
        compiler_params=_params("arbitrary"),
        name="ffn_ln",
    )(x2d, w_in_bf16, w_out_bf16, g[None, :], b[None, :])


def kernel(x, positions, ev_w_in, ev_ret_norm_g, ev_pool_w, ev_pool_scale, ev_w_out, od_w_in, od_conv_w, od_conv_b, od_gate_a_w, od_gate_a_b, od_gate_x_w, od_gate_x_b, od_lru_lambda, od_w_out, ffn_w_in, ffn_w_out, ln_g, ln_b):
    B, S, D = x.shape
    T = B * S
    cos, sin = _rope_tables(positions)
    h = x.reshape(T, D)
    for layer in range(DEPTH):
        j = layer // 2
        if layer % 2 == 0:
            z = _in_proj(h, ev_w_in[j].astype(BF16)).reshape(B, S, EVEN_IN)
            a1, a2 = _even_core(z, cos, sin, ev_ret_norm_g[j], ev_pool_w[j].astype(BF16), ev_pool_scale[j])
            w_out = ev_w_out[j]
        else:
            z = _in_proj(h, od_w_in[j].astype(BF16)).reshape(B, S, ODD_IN)
            a1 = _lru_core(z, od_conv_w[j], od_conv_b[j], od_gate_a_w[j], od_gate_a_b[j],
                           od_gate_x_w[j], od_gate_x_b[j], od_lru_lambda[j])
            a2 = _att_core(z, cos, sin)
            w_out = od_w_out[j]
        h = _out_ln(h, a1.reshape(T, HALF_WIDTH), a2.reshape(T, HALF_WIDTH), w_out.astype(BF16),
                    ln_g[layer, 0], ln_b[layer, 0])
        h = _ffn_ln(h, ffn_w_in[layer].astype(BF16), ffn_w_out[layer].astype(BF16), ln_g[layer, 1], ln_b[layer, 1])
    return h.reshape(B, S, D)
```

```python
import functools
import math

import jax
import jax.numpy as jnp
import numpy as np
from jax import lax
from jax.experimental import pallas as pl
from jax.experimental.pallas import tpu as pltpu

F32 = jnp.float32
BF16 = jnp.bfloat16

D_MODEL = 1024
DEPTH = 4
HEAD_DIM = 128
HALF_WIDTH = 512
N_HEADS = HALF_WIDTH // HEAD_DIM
RET_CHUNK = 128
POOL_WINDOWS = (2, 4, 8, 16)
POOL_GROUP = 128
POOL_HALO = 32
LRU_BLOCKS = 8
LRU_C = 8.0
CONV_WIDTH = 4
CONV_HALO = 8
DIL_PATTERNS = ((128, 1), (512, 4), (2048, 16))
ATT_BLOCK = 128
ATT_TILE = 2048
ATT_RES = 16
ROPE_THETA = 10000.0
EVEN_IN = 5 * HALF_WIDTH
ODD_IN = 5 * HALF_WIDTH
D_FF = 2816
DEEPNORM_ALPHA = (2 * DEPTH) ** 0.25
LN_EPS = 1e-5

VMEM_LIMIT = 56 * 1024 * 1024


def _params(*sem):
    return pltpu.CompilerParams(dimension_semantics=sem, vmem_limit_bytes=VMEM_LIMIT)


def _sigmoid(x):
    return 1.0 / (1.0 + jnp.exp(-x))


def _silu(x):
    return x * _sigmoid(x)


def _gelu_tanh(x):
    c = math.sqrt(2.0 / math.pi)
    return x * (0.5 * (1.0 + jnp.tanh(c * (x + 0.044715 * (x * x * x)))))


def _layer_norm(y, g, b):
    mu = jnp.mean(y, axis=-1, keepdims=True)
    d = y - mu
    var = jnp.mean(d * d, axis=-1, keepdims=True)
    return d * lax.rsqrt(var + LN_EPS) * g + b


def _rope(t, cos, sin_signed):
    return t * cos + pltpu.roll(t, HEAD_DIM // 2, axis=1) * sin_signed


def _dot(a, b):
    return jnp.dot(a, b, preferred_element_type=F32)


def _dot_nt(a, b):
    return lax.dot_general(a, b, (((1,), (1,)), ((), ())), preferred_element_type=F32)


def _rope_table_kernel(pos_ref, inv_ref, sign_ref, cos_ref, sin_ref):
    ang = pos_ref[...].astype(F32) * inv_ref[...]
    cos_ref[...] = jnp.cos(ang)
    sin_ref[...] = jnp.sin(ang) * sign_ref[...]


def _rope_tables(positions):
    B, S = positions.shape
    T = B * S
    ts = 2048
    half = HEAD_DIM // 2
    inv = ROPE_THETA ** (-jnp.arange(0, HEAD_DIM, 2, dtype=F32) / HEAD_DIM)
    inv2 = jnp.concatenate([inv, inv])[None, :]
    sign = jnp.concatenate([-jnp.ones((half,), F32), jnp.ones((half,), F32)])[None, :]
    row = pl.BlockSpec((1, HEAD_DIM), lambda i: (0, 0))
    cos, sin = pl.pallas_call(
        _rope_table_kernel,
        grid=(T // ts,),
        in_specs=[pl.BlockSpec((ts, 1), lambda i: (i, 0)), row, row],
        out_specs=[pl.BlockSpec((ts, HEAD_DIM), lambda i: (i, 0))] * 2,
        out_shape=[jax.ShapeDtypeStruct((T, HEAD_DIM), F32)] * 2,
        compiler_params=_params("arbitrary"),
        name="rope_tables",
    )(positions.reshape(T, 1), inv2, sign)
    return cos.reshape(B, S, HEAD_DIM), sin.reshape(B, S, HEAD_DIM)


def _proj_kernel(x_ref, w_ref, z_ref):
    z_ref[...] = _dot(x_ref[...].astype(BF16), w_ref[...])


def _in_proj(x2d, w_bf16):
    T, D = x2d.shape
    N = w_bf16.shape[1]
    tm = 512
    return pl.pallas_call(
        _proj_kernel,
        grid=(T // tm,),
        in_specs=[pl.BlockSpec((tm, D), lambda i: (i, 0)), pl.BlockSpec((D, N), lambda i: (0, 0))],
        out_specs=pl.BlockSpec((tm, N), lambda i: (i, 0)),
        out_shape=jax.ShapeDtypeStruct((T, N), F32),
        compiler_params=_params("arbitrary"),
        name="in_proj",
    )(x2d, w_bf16)


def _even_kernel(z_ref, cos_ref, sin_ref, decay_ref, qdec_ref, kdec_ref, cdec_ref, gn_ref, pw_ref, ps_ref,
                 ret_ref, pool_ref, state_ref, pbuf_ref, s2_ref, s4_ref, s8_ref, *, ts):
    sblk = pl.program_id(1)
    H = POOL_HALO

    @pl.when(sblk == 0)
    def _():
        state_ref[...] = jnp.zeros_like(state_ref)
        pbuf_ref[0:H, :] = jnp.zeros((H, HALF_WIDTH), F32)

    for c in range(ts // RET_CHUNK):
        rows = pl.ds(c * RET_CHUNK, RET_CHUNK)
        cos = cos_ref[rows, :]
        sin = sin_ref[rows, :]
        for h in range(N_HEADS):
            lanes = slice(h * HEAD_DIM, (h + 1) * HEAD_DIM)
            q = _rope(z_ref[rows, lanes], cos, sin)
            k = _rope(z_ref[rows, pl.ds(HALF_WIDTH + h * HEAD_DIM, HEAD_DIM)], cos, sin) * (HEAD_DIM ** -0.5)
            v = z_ref[rows, pl.ds(2 * HALF_WIDTH + h * HEAD_DIM, HEAD_DIM)]
            gate = z_ref[rows, pl.ds(3 * HALF_WIDTH + h * HEAD_DIM, HEAD_DIM)]
            qb = q.astype(BF16)
            kb = k.astype(BF16)
            vb = v.astype(BF16)
            scores = _dot_nt(qb, kb) * decay_ref[h]
            inner = _dot(scores.astype(BF16), vb)
            state = state_ref[h]
            cross = _dot(qb, state.astype(BF16)) * qdec_ref[h]
            kd = (k * kdec_ref[h]).T.astype(BF16)
            state_ref[h] = state * cdec_ref[h] + _dot(kd, vb)
            y = inner + cross
            mu = jnp.mean(y, axis=-1, keepdims=True)
            d = y - mu
            var = jnp.mean(d * d, axis=-1, keepdims=True)
            yn = d * lax.rsqrt(var + LN_EPS) * gn_ref[:, lanes]
            ret_ref[rows, lanes] = (yn * _silu(gate)).astype(ret_ref.dtype)

    pbuf_ref[H:H + ts, :] = z_ref[:, pl.ds(4 * HALF_WIDTH, HALF_WIDTH)]
    G = POOL_GROUP
    s2_ref[8:H + ts, :] = pbuf_ref[8:H + ts, :] + pbuf_ref[7:H + ts - 1, :]
    s4_ref[16:H + ts, :] = s2_ref[16:H + ts, G:4 * G] + s2_ref[14:H + ts - 2, G:4 * G]
    s8_ref[24:H + ts, :] = s4_ref[24:H + ts, G:3 * G] + s4_ref[20:H + ts - 4, G:3 * G]
    s16 = s8_ref[32:H + ts, G:2 * G] + s8_ref[24:H + ts - 8, G:2 * G]
    wsums = (s2_ref[H:H + ts, 0:G], s4_ref[H:H + ts, 0:G], s8_ref[H:H + ts, 0:G], s16)
    tpos = sblk * ts + lax.broadcasted_iota(jnp.int32, (ts, 1), 0) + 1
    for gi, w in enumerate(POOL_WINDOWS):
        lanes = slice(gi * G, (gi + 1) * G)
        cnt = jnp.minimum(tpos, w).astype(F32)
        pooled = wsums[gi] / cnt - pbuf_ref[H:H + ts, lanes]
        mixed = _dot(pooled.astype(BF16), pw_ref[gi]) * ps_ref[:, lanes]
        pool_ref[:, lanes] = mixed.astype(pool_ref.dtype)
    pbuf_ref[0:H, :] = pbuf_ref[ts:ts + H, :]


def _even_core(z, cos, sin, ret_norm_g, pool_w_bf16, pool_scale):
    B, S, _ = z.shape
    ts = 512
    C = RET_CHUNK
    lg = jnp.log1p(-(2.0 ** (-5.0 - jnp.arange(N_HEADS, dtype=F32))))
    idx = jnp.arange(C, dtype=F32)
    rel = idx[:, None] - idx[None, :]
    decay = jnp.where(rel[None] >= 0, jnp.exp(jnp.maximum(rel, 0.0)[None] * lg[:, None, None]), 0.0)
    k_decay = jnp.exp((C - 1 - idx)[None, :] * lg[:, None])
    q_decay = jnp.exp((idx + 1.0)[None, :] * lg[:, None])
    chunk_decay = jnp.exp(C * lg)
    qdec = jnp.broadcast_to(q_decay[:, :, None], (N_HEADS, C, HEAD_DIM))
    kdec = jnp.broadcast_to(k_decay[:, :, None], (N_HEADS, C, HEAD_DIM))
    cdec = jnp.broadcast_to(chunk_decay[:, None, None], (N_HEADS, 1, HEAD_DIM))

    const3 = lambda shape: pl.BlockSpec(shape, lambda b, i: (0, 0, 0))
    const2 = lambda shape: pl.BlockSpec(shape, lambda b, i: (0, 0))
    seq = lambda w: pl.BlockSpec((None, ts, w), lambda b, i: (b, i, 0))
    return pl.pallas_call(
        functools.partial(_even_kernel, ts=ts),
        grid=(B, S // ts),
        in_specs=[seq(EVEN_IN), seq(HEAD_DIM), seq(HEAD_DIM),
                  const3((N_HEADS, C, C)), const3((N_HEADS, C, HEAD_DIM)), const3((N_HEADS, C, HEAD_DIM)),
                  const3((N_HEADS, 1, HEAD_DIM)), const2((1, HALF_WIDTH)),
                  const3((len(POOL_WINDOWS), POOL_GROUP, POOL_GROUP)), const2((1, HALF_WIDTH))],
        out_specs=[seq(HALF_WIDTH), seq(HALF_WIDTH)],
        out_shape=[jax.ShapeDtypeStruct((B, S, HALF_WIDTH), BF16)] * 2,
        scratch_shapes=[pltpu.VMEM((N_HEADS, HEAD_DIM, HEAD_DIM), F32),
                        pltpu.VMEM((POOL_HALO + ts, HALF_WIDTH), F32),
                        pltpu.VMEM((POOL_HALO + ts, HALF_WIDTH), F32),
                        pltpu.VMEM((POOL_HALO + ts, 3 * POOL_GROUP), F32),
                        pltpu.VMEM((POOL_HALO + ts, 2 * POOL_GROUP), F32)],
        compiler_params=_params("arbitrary", "arbitrary"),
        name="even_core",
    )(z, cos, sin, decay, qdec, kdec, cdec, ret_norm_g[None, :], pool_w_bf16, pool_scale[None, :])


def _lru_kernel(z_ref, cw_ref, cb_ref, wa_ref, ba_ref, wx_ref, bx_ref, lam_ref, y_ref, ubuf_ref, hcar_ref, *, ts):
    sblk = pl.program_id(1)
    H = CONV_HALO

    @pl.when(sblk == 0)
    def _():
        ubuf_ref[0:H, :] = jnp.zeros((H, HALF_WIDTH), F32)
        hcar_ref[...] = jnp.zeros_like(hcar_ref)

    ubuf_ref[H:H + ts, :] = z_ref[:, pl.ds(HALF_WIDTH, HALF_WIDTH)]
    u = cb_ref[...] + cw_ref[CONV_WIDTH - 1:CONV_WIDTH, :] * ubuf_ref[H:H + ts, :]
    for k in range(CONV_WIDTH - 1):
        off = H - (CONV_WIDTH - 1) + k
        u = u + cw_ref[k:k + 1, :] * ubuf_ref[off:off + ts, :]
    ubuf_ref[0:H, :] = ubuf_ref[ts:ts + H, :]

    ub = u.astype(BF16)
    r = _sigmoid(_dot(ub, wa_ref[...]) + ba_ref[...])
    ig = _sigmoid(_dot(ub, wx_ref[...]) + bx_ref[...])
    nl = -lam_ref[...]
    softplus = jnp.maximum(nl, 0.0) + jnp.log1p(jnp.exp(-jnp.abs(nl)))
    log_a = (-LRU_C * r) * softplus
    a = jnp.exp(log_a)
    bseq = jnp.sqrt(jnp.tanh(-log_a) * (1.0 + a * a)) * (ig * u)

    row = lax.broadcasted_iota(jnp.int32, (ts, 1), 0)
    sh = 1
    while sh < ts:
        a_prev = pltpu.roll(a, sh, axis=0)
        b_prev = pltpu.roll(bseq, sh, axis=0)
        m = row >= sh
        bseq = jnp.where(m, a * b_prev + bseq, bseq)
        a = jnp.where(m, a * a_prev, a)
        sh *= 2
    h = bseq + a * hcar_ref[0:1, :]
    hcar_ref[...] = jnp.broadcast_to(h[ts - 1:ts, :], hcar_ref.shape)
    y_ref[...] = (h * _gelu_tanh(z_ref[:, 0:HALF_WIDTH])).astype(y_ref.dtype)


def _block_diag(w):
    n, c, d = w.shape
    eye = jnp.eye(n, dtype=w.dtype)
    return (w[:, :, None, :] * eye[:, None, :, None]).reshape(n * c, n * d)


def _lru_core(z, conv_w, conv_b, gate_a_w, gate_a_b, gate_x_w, gate_x_b, lam):
    B, S, _ = z.shape
    ts = 512
    W = HALF_WIDTH
    wa = _block_diag(gate_a_w).astype(BF16)
    wx = _block_diag(gate_x_w).astype(BF16)
    const = lambda shape: pl.BlockSpec(shape, lambda b, i: (0, 0))
    return pl.pallas_call(
        functools.partial(_lru_kernel, ts=ts),
        grid=(B, S // ts),
        in_specs=[pl.BlockSpec((None, ts, 2 * W), lambda b, i: (b, i, 0)),
                  const((CONV_WIDTH, W)), const((1, W)), const((W, W)), const((1, W)),
                  const((W, W)), const((1, W)), const((1, W))],
        out_specs=pl.BlockSpec((None, ts, W), lambda b, i: (b, i, 0)),
        out_shape=jax.ShapeDtypeStruct((B, S, W), BF16),
        scratch_shapes=[pltpu.VMEM((CONV_HALO + ts, W), F32), pltpu.VMEM((8, W), F32)],
        compiler_params=_params("arbitrary", "arbitrary"),
        name="lru_core",
    )(z, conv_w, conv_b[None, :], wa, gate_a_b[None, :], wx, gate_x_b[None, :], lam[None, :])


def _qkv_rm_kernel(x_ref, w_ref, cos_ref, sin_ref, q_ref, k_ref, v_ref, *, nres):
    D = D_MODEL
    rows_per = x_ref.shape[0]
    x = jnp.concatenate([x_ref[:, j * D:(j + 1) * D] for j in range(nres)], axis=0).astype(BF16)
    z = _dot(x, w_ref[...])
    qscale = HEAD_DIM ** -0.5 * math.log2(math.e)
    for j in range(nres):
        rows = slice(j * rows_per, (j + 1) * rows_per)
        cos = cos_ref[:, j * HEAD_DIM:(j + 1) * HEAD_DIM]
        sin = sin_ref[:, j * HEAD_DIM:(j + 1) * HEAD_DIM]
        for h in range(N_HEADS):
            lanes = slice(h * HEAD_DIM, (h + 1) * HEAD_DIM)
            q_ref[j, :, lanes] = _rope(z[rows, h * HEAD_DIM:(h + 1) * HEAD_DIM], cos, sin) * qscale
            k_ref[j, :, lanes] = _rope(z[rows, HALF_WIDTH + h * HEAD_DIM:HALF_WIDTH + (h + 1) * HEAD_DIM], cos, sin)
        v_ref[j] = z[rows, 2 * HALF_WIDTH:3 * HALF_WIDTH]


def _qkv_rm(x3, w_bf16, cos, sin):
    B, S, D = x3.shape
    R = ATT_RES
    I = S // R
    nres = 4
    W = HALF_WIDTH
    out_spec = pl.BlockSpec((None, nres, I, W), lambda b, g: (b, g, 0, 0))
    return pl.pallas_call(
        functools.partial(_qkv_rm_kernel, nres=nres),
        grid=(B, R // nres),
        in_specs=[pl.BlockSpec((None, I, nres * D), lambda b, g: (b, 0, g)),
                  pl.BlockSpec((D, 3 * W), lambda b, g: (0, 0), pipeline_mode=pl.Buffered(1)),
                  pl.BlockSpec((None, I, nres * HEAD_DIM), lambda b, g: (b, 0, g)),
                  pl.BlockSpec((None, I, nres * HEAD_DIM), lambda b, g: (b, 0, g))],
        out_specs=[out_spec] * 3,
        out_shape=[jax.ShapeDtypeStruct((B, R, I, W), F32)] * 3,
        compiler_params=_params("arbitrary", "arbitrary"),
        name="qkv_rm",
    )(x3.reshape(B, I, R * D), w_bf16, cos.reshape(B, I, R * HEAD_DIM), sin.reshape(B, I, R * HEAD_DIM))


def _pattern_bias(dil):
    nslab = ATT_RES // dil
    slab = ATT_BLOCK // nslab
    row = np.arange(ATT_BLOCK)
    col = np.arange(2 * ATT_BLOCK)
    jq = nslab * (row % slab) + row // slab + ATT_BLOCK
    jk = nslab * (col % slab) + (col % ATT_BLOCK) // slab + ATT_BLOCK * (col // ATT_BLOCK)
    delta = jq[:, None] - jk[None, :]
    return np.where((delta >= 0) & (delta <= ATT_BLOCK), 0.0, -np.inf).astype(np.float32)


def _att_rm_kernel(q_ref, k_ref, v_ref, bias_ref, o_ref, num_ref, den_ref, mx_ref):
    QB = ATT_BLOCK
    TI = ATT_TILE // ATT_RES
    ones = jnp.ones((2 * QB, HEAD_DIM), BF16)

    def gather(ref, slabs, r0, rn):
        return jnp.concatenate([ref[r, r0:r0 + rn, :] for r in slabs], axis=0)

    for tile in range(q_ref.shape[1] // TI):
        i0 = tile * TI
        for p, (window, dil) in enumerate(DIL_PATTERNS):
            assert window // dil == QB
            nslab = ATT_RES // dil
            slab = QB // nslab
            for n in range(ATT_TILE // QB):
                res, m = n % dil, n // dil
                slabs = [res + dil * a for a in range(nslab)]
                c0 = i0 + slab * m
                qb = gather(q_ref, slabs, c0, slab).astype(BF16)
                if c0 == 0:
                    kb = gather(k_ref, slabs, c0, slab).astype(BF16)
                    vb = gather(v_ref, slabs, c0, slab).astype(BF16)
                    bias = bias_ref[p, :, QB:2 * QB]
                else:
                    kb = jnp.concatenate([gather(k_ref, slabs, c0 - slab, slab),
                                          gather(k_ref, slabs, c0, slab)], axis=0).astype(BF16)
                    vb = jnp.concatenate([gather(v_ref, slabs, c0 - slab, slab),
                                          gather(v_ref, slabs, c0, slab)], axis=0).astype(BF16)
                    bias = bias_ref[p]
                s = _dot_nt(qb, kb) + bias
                mx = jnp.max(s, axis=-1, keepdims=True)
                e = jnp.exp2(s - mx).astype(BF16)
                nd = _dot(e, jnp.concatenate([vb, ones[0:vb.shape[0]]], axis=1))
                mxb = jnp.broadcast_to(mx, (QB, HEAD_DIM))
                for a, r in enumerate(slabs):
                    rows = slice(a * slab, (a + 1) * slab)
                    dst = slice(slab * m, slab * (m + 1))
                    num_ref[p, r, dst, :] = nd[rows, 0:HEAD_DIM]
                    den_ref[p, r, dst, :] = nd[rows, HEAD_DIM:2 * HEAD_DIM]
                    mx_ref[p, r, dst, :] = mxb[rows]

        m0, m1, m2 = mx_ref[0], mx_ref[1], mx_ref[2]
        mm = jnp.maximum(jnp.maximum(m0, m1), m2)
        w0, w1, w2 = jnp.exp2(m0 - mm), jnp.exp2(m1 - mm), jnp.exp2(m2 - mm)
        num = w0 * num_ref[0] + w1 * num_ref[1] + w2 * num_ref[2]
        den = w0 * den_ref[0] + w1 * den_ref[1] + w2 * den_ref[2]
        o_ref[:, i0:i0 + TI, :] = (num / den).astype(o_ref.dtype)


def _att_rm(q, k, v):
    B, R, I, W = q.shape
    TI = ATT_TILE // ATT_RES
    assert R == ATT_RES and I % TI == 0
    bias = jnp.asarray(np.stack([_pattern_bias(dil) for _, dil in DIL_PATTERNS]))
    blk = pl.BlockSpec((None, R, I, HEAD_DIM), lambda b, h: (b, 0, 0, h))
    acc = pltpu.VMEM((len(DIL_PATTERNS), R, TI, HEAD_DIM), F32)
    return pl.pallas_call(
        _att_rm_kernel,
        grid=(B, N_HEADS),
        in_specs=[blk, blk, blk,
                  pl.BlockSpec(bias.shape, lambda b, h: (0, 0, 0), pipeline_mode=pl.Buffered(1))],
        out_specs=blk,
        out_shape=jax.ShapeDtypeStruct((B, R, I, W), BF16),
        scratch_shapes=[acc, acc, acc],
        compiler_params=_params("arbitrary", "arbitrary"),
        name="att_rm",
    )(q, k, v, bias)


def _out_ln_kernel(x_ref, a1_ref, a2_ref, w_ref, g_ref, b_ref, o_ref):
    W = HALF_WIDTH
    mix = _dot(a1_ref[...], w_ref[0:W, :]) + _dot(a2_ref[...], w_ref[W:2 * W, :])
    o_ref[...] = _layer_norm(DEEPNORM_ALPHA * x_ref[...] + mix, g_ref[...], b_ref[...])


def _out_ln(x2d, a1, a2, w_bf16, g, b):
    T, D = x2d.shape
    tm = 512
    rowblk = lambda w: pl.BlockSpec((tm, w), lambda i: (i, 0))
    const = lambda shape: pl.BlockSpec(shape, lambda i: (0, 0))
    return pl.pallas_call(
        _out_ln_kernel,
        grid=(T // tm,),
        in_specs=[rowblk(D), rowblk(HALF_WIDTH), rowblk(HALF_WIDTH), const((D, D)), const((1, D)), const((1, D))],
        out_specs=rowblk(D),
        out_shape=jax.ShapeDtypeStruct((T, D), F32),
        compiler_params=_params("arbitrary"),
        name="out_ln",
    )(x2d, a1, a2, w_bf16, g[None, :], b[None, :])


def _out_ln_rm_kernel(x_ref, a1_ref, a2_ref, w_ref, g_ref, b_ref, o_ref, *, nres):
    D = D_MODEL
    W = HALF_WIDTH
    rows_per = x_ref.shape[0]
    a1 = jnp.concatenate([a1_ref[:, j * W:(j + 1) * W] for j in range(nres)], axis=0)
    a2 = jnp.concatenate([a2_ref[j] for j in range(nres)], axis=0)
    mix = _dot(a1, w_ref[0:W, :]) + _dot(a2, w_ref[W:2 * W, :])
    for j in range(nres):
        lanes = slice(j * D, (j + 1) * D)
        y = DEEPNORM_ALPHA * x_ref[:, lanes] + mix[j * rows_per:(j + 1) * rows_per]
        o_ref[:, lanes] = _layer_norm(y, g_ref[...], b_ref[...])


def _out_ln_rm(x3, a1, a2_rm, w_bf16, g, b):
    B, S, D = x3.shape
    R = ATT_RES
    I = S // R
    W = HALF_WIDTH
    nres = 4
    const = lambda shape: pl.BlockSpec(shape, lambda b, g: (0, 0), pipeline_mode=pl.Buffered(1))
    out = pl.pallas_call(
        functools.partial(_out_ln_rm_kernel, nres=nres),
        grid=(B, R // nres),
        in_specs=[pl.BlockSpec((None, I, nres * D), lambda b, g: (b, 0, g)),
                  pl.BlockSpec((None, I, nres * W), lambda b, g: (b, 0, g)),
                  pl.BlockSpec((None, nres, I, W), lambda b, g: (b, g, 0, 0)),
                  const((D, D)), const((1, D)), const((1, D))],
        out_specs=pl.BlockSpec((None, I, nres * D), lambda b, g: (b, 0, g)),
        out_shape=jax.ShapeDtypeStruct((B, I, R * D), F32),
        compiler_params=_params("arbitrary", "arbitrary"),
        name="out_ln_rm",
    )(x3.reshape(B, I, R * D), a1.reshape(B, I, R * W), a2_rm, w_bf16, g[None, :], b[None, :])
    return out.reshape(B * S, D)


FFN_SUB = 512
FFN_CHUNK = 256


def _ffn_kernel(x_ref, wi_ref, wo_ref, g_ref, b_ref, o_ref, *, tm):
    for r0 in range(0, tm, FFN_SUB):
        rows = slice(r0, r0 + FFN_SUB)
        xb = x_ref[rows, :].astype(BF16)
        acc = None
        for c0 in range(0, D_FF, FFN_CHUNK):
            gate = _dot(xb, wi_ref[:, c0:c0 + FFN_CHUNK])
            up = _dot(xb, wi_ref[:, D_FF + c0:D_FF + c0 + FFN_CHUNK])
            part = _dot((_silu(gate) * up).astype(BF16), wo_ref[c0:c0 + FFN_CHUNK, :])
            acc = part if acc is None else acc + part
        o_ref[rows, :] = _layer_norm(DEEPNORM_ALPHA * x_ref[rows, :] + acc, g_ref[...], b_ref[...])


def _ffn_ln(x2d, w_in_bf16, w_out_bf16, g, b):
    T, D = x2d.shape
    tm = 1024
    resident = lambda shape: pl.BlockSpec(shape, lambda i: (0, 0), pipeline_mode=pl.Buffered(1))
    return pl.pallas_call(
        functools.partial(_ffn_kernel, tm=tm),
        grid=(T // tm,),
        in_specs=[pl.BlockSpec((tm, D), lambda i: (i, 0)),
                  resident((D, 2 * D_FF)), resident((D_FF, D)), resident((1, D)), resident((1, D))],
        out_specs=pl.BlockSpec((tm, D), lambda i: (i, 0)),
        out_shape=jax.ShapeDtypeStruct((T, D), F32),
        compiler_params=_params("arbitrary"),
        name="ffn_ln",
    )(x2d, w_in_bf16, w_out_bf16, g[None, :], b[None, :])


def kernel(x, positions, ev_w_in, ev_ret_norm_g, ev_pool_w, ev_pool_scale, ev_w_out, od_w_in, od_conv_w, od_conv_b, od_gate_a_w, od_gate_a_b, od_gate_x_w, od_gate_x_b, od_lru_lambda, od_w_out, ffn_w_in, ffn_w_out, ln_g, ln_b):
    B, S, D = x.shape
    T = B * S
    cos, sin = _rope_tables(positions)
    h = x.reshape(T, D)
    for layer in range(DEPTH):
        j = layer // 2
        if layer % 2 == 0:
            z = _in_proj(h, ev_w_in[j].astype(BF16)).reshape(B, S, EVEN_IN)
            a1, a2 = _even_core(z, cos, sin, ev_ret_norm_g[j], ev_pool_w[j].astype(BF16), ev_pool_scale[j])
            h = _out_ln(h, a1.reshape(T, HALF_WIDTH), a2.reshape(T, HALF_WIDTH), ev_w_out[j].astype(BF16),
                        ln_g[layer, 0], ln_b[layer, 0])
        else:
            w_in = od_w_in[j].astype(BF16)
            z = _in_proj(h, w_in[:, :2 * HALF_WIDTH]).reshape(B, S, 2 * HALF_WIDTH)
            a1 = _lru_core(z, od_conv_w[j], od_conv_b[j], od_gate_a_w[j], od_gate_a_b[j],
                           od_gate_x_w[j], od_gate_x_b[j], od_lru_lambda[j])
            a2 = _att_rm(*_qkv_rm(h.reshape(B, S, D), w_in[:, 2 * HALF_WIDTH:], cos, sin))
            h = _out_ln_rm(h.reshape(B, S, D), a1, a2, od_w_out[j].astype(BF16), ln_g[layer, 0], ln_b[layer, 0])
        h = _ffn_ln(h, ffn_w_in[layer].astype(BF16), ffn_w_out[layer].astype(BF16), ln_g[layer, 1], ln_b[layer, 1])
    return h.reshape(B, S, D)
```

```python
import functools
import math

import jax
import jax.numpy as jnp
import numpy as np
from jax import lax
from jax.experimental import pallas as pl
from jax.experimental.pallas import tpu as pltpu

F32 = jnp.float32
BF16 = jnp.bfloat16

D_MODEL = 1024
DEPTH = 4
HEAD_DIM = 128
HALF_WIDTH = 512
N_HEADS = HALF_WIDTH // HEAD_DIM
RET_CHUNK = 128
POOL_WINDOWS = (2, 4, 8, 16)
POOL_GROUP = 128
POOL_HALO = 32
LRU_BLOCKS = 8
LRU_C = 8.0
CONV_WIDTH = 4
CONV_HALO = 8
DIL_PATTERNS = ((128, 1), (512, 4), (2048, 16))
ATT_BLOCK = 128
ATT_TILE = 2048
ATT_RES = 16
ROPE_THETA = 10000.0
EVEN_IN = 5 * HALF_WIDTH
ODD_IN = 5 * HALF_WIDTH
D_FF = 2816
DEEPNORM_ALPHA = (2 * DEPTH) ** 0.25
LN_EPS = 1e-5

VMEM_LIMIT = 56 * 1024 * 1024


def _params(*sem):
    return pltpu.CompilerParams(dimension_semantics=sem, vmem_limit_bytes=VMEM_LIMIT)


def _sigmoid(x):
    return 1.0 / (1.0 + jnp.exp(-x))


def _silu(x):
    return x * _sigmoid(x)


def _gelu_tanh(x):
    c = math.sqrt(2.0 / math.pi)
    return x * (0.5 * (1.0 + jnp.tanh(c * (x + 0.044715 * (x * x * x)))))


def _layer_norm(y, g, b):
    mu = jnp.mean(y, axis=-1, keepdims=True)
    d = y - mu
    var = jnp.mean(d * d, axis=-1, keepdims=True)
    return d * lax.rsqrt(var + LN_EPS) * g + b


def _rope(t, cos, sin_signed):
    return t * cos + pltpu.roll(t, HEAD_DIM // 2, axis=1) * sin_signed


def _dot(a, b):
    return jnp.dot(a, b, preferred_element_type=F32)


def _dot_nt(a, b):
    return lax.dot_general(a, b, (((1,), (1,)), ((), ())), preferred_element_type=F32)


def _rope_table_kernel(pos_ref, inv_ref, sign_ref, cos_ref, sin_ref):
    ang = pos_ref[...].astype(F32) * inv_ref[...]
    cos_ref[...] = jnp.cos(ang)
    sin_ref[...] = jnp.sin(ang) * sign_ref[...]


def _rope_tables(positions):
    B, S = positions.shape
    T = B * S
    ts = 2048
    half = HEAD_DIM // 2
    inv = ROPE_THETA ** (-jnp.arange(0, HEAD_DIM, 2, dtype=F32) / HEAD_DIM)
    inv2 = jnp.concatenate([inv, inv])[None, :]
    sign = jnp.concatenate([-jnp.ones((half,), F32), jnp.ones((half,), F32)])[None, :]
    row = pl.BlockSpec((1, HEAD_DIM), lambda i: (0, 0))
    cos, sin = pl.pallas_call(
        _rope_table_kernel,
        grid=(T // ts,),
        in_specs=[pl.BlockSpec((ts, 1), lambda i: (i, 0)), row, row],
        out_specs=[pl.BlockSpec((ts, HEAD_DIM), lambda i: (i, 0))] * 2,
        out_shape=[jax.ShapeDtypeStruct((T, HEAD_DIM), F32)] * 2,
        compiler_params=_params("arbitrary"),
        name="rope_tables",
    )(positions.reshape(T, 1), inv2, sign)
    return cos.reshape(B, S, HEAD_DIM), sin.reshape(B, S, HEAD_DIM)


def _proj_kernel(x_ref, w_ref, z_ref):
    z_ref[...] = _dot(x_ref[...].astype(BF16), w_ref[...])


def _in_proj(x2d, w_bf16):
    T, D = x2d.shape
    N = w_bf16.shape[1]
    tm = 512
    return pl.pallas_call(
        _proj_kernel,
        grid=(T // tm,),
        in_specs=[pl.BlockSpec((tm, D), lambda i: (i, 0)), pl.BlockSpec((D, N), lambda i: (0, 0))],
        out_specs=pl.BlockSpec((tm, N), lambda i: (i, 0)),
        out_shape=jax.ShapeDtypeStruct((T, N), F32),
        compiler_params=_params("arbitrary"),
        name="in_proj",
    )(x2d, w_bf16)


def _even_kernel(z_ref, cos_ref, sin_ref, decay_ref, qdec_ref, kdec_ref, cdec_ref, gn_ref, pw_ref, ps_ref,
                 ret_ref, pool_ref, state_ref, pbuf_ref, s2_ref, s4_ref, s8_ref, *, ts):
    sblk = pl.program_id(1)
    H = POOL_HALO

    @pl.when(sblk == 0)
    def _():
        state_ref[...] = jnp.zeros_like(state_ref)
        pbuf_ref[0:H, :] = jnp.zeros((H, HALF_WIDTH), F32)

    for c in range(ts // RET_CHUNK):
        rows = pl.ds(c * RET_CHUNK, RET_CHUNK)
        cos = cos_ref[rows, :]
        sin = sin_ref[rows, :]
        for h in range(N_HEADS):
            lanes = slice(h * HEAD_DIM, (h + 1) * HEAD_DIM)
            q = _rope(z_ref[rows, lanes], cos, sin)
            k = _rope(z_ref[rows, pl.ds(HALF_WIDTH + h * HEAD_DIM, HEAD_DIM)], cos, sin) * (HEAD_DIM ** -0.5)
            v = z_ref[rows, pl.ds(2 * HALF_WIDTH + h * HEAD_DIM, HEAD_DIM)]
            gate = z_ref[rows, pl.ds(3 * HALF_WIDTH + h * HEAD_DIM, HEAD_DIM)]
            qb = q.astype(BF16)
            kb = k.astype(BF16)
            vb = v.astype(BF16)
            scores = _dot_nt(qb, kb) * decay_ref[h]
            inner = _dot(scores.astype(BF16), vb)
            state = state_ref[h]
            cross = _dot(qb, state.astype(BF16)) * qdec_ref[h]
            kd = (k * kdec_ref[h]).T.astype(BF16)
            state_ref[h] = state * cdec_ref[h] + _dot(kd, vb)
            y = inner + cross
            mu = jnp.mean(y, axis=-1, keepdims=True)
            d = y - mu
            var = jnp.mean(d * d, axis=-1, keepdims=True)
            yn = d * lax.rsqrt(var + LN_EPS) * gn_ref[:, lanes]
            ret_ref[rows, lanes] = (yn * _silu(gate)).astype(ret_ref.dtype)

    pbuf_ref[H:H + ts, :] = z_ref[:, pl.ds(4 * HALF_WIDTH, HALF_WIDTH)]
    G = POOL_GROUP
    s2_ref[8:H + ts, :] = pbuf_ref[8:H + ts, :] + pbuf_ref[7:H + ts - 1, :]
    s4_ref[16:H + ts, :] = s2_ref[16:H + ts, G:4 * G] + s2_ref[14:H + ts - 2, G:4 * G]
    s8_ref[24:H + ts, :] = s4_ref[24:H + ts, G:3 * G] + s4_ref[20:H + ts - 4, G:3 * G]
    s16 = s8_ref[32:H + ts, G:2 * G] + s8_ref[24:H + ts - 8, G:2 * G]
    wsums = (s2_ref[H:H + ts, 0:G], s4_ref[H:H + ts, 0:G], s8_ref[H:H + ts, 0:G], s16)
    tpos = sblk * ts + lax.broadcasted_iota(jnp.int32, (ts, 1), 0) + 1
    for gi, w in enumerate(POOL_WINDOWS):
        lanes = slice(gi * G, (gi + 1) * G)
        cnt = jnp.minimum(tpos, w).astype(F32)
        pooled = wsums[gi] / cnt - pbuf_ref[H:H + ts, lanes]
        mixed = _dot(pooled.astype(BF16), pw_ref[gi]) * ps_ref[:, lanes]
        pool_ref[:, lanes] = mixed.astype(pool_ref.dtype)
    pbuf_ref[0:H, :] = pbuf_ref[ts:ts + H, :]


def _even_core(z, cos, sin, ret_norm_g, pool_w_bf16, pool_scale):
    B, S, _ = z.shape
    ts = 512
    C = RET_CHUNK
    lg = jnp.log1p(-(2.0 ** (-5.0 - jnp.arange(N_HEADS, dtype=F32))))
    idx = jnp.arange(C, dtype=F32)
    rel = idx[:, None] - idx[None, :]
    decay = jnp.where(rel[None] >= 0, jnp.exp(jnp.maximum(rel, 0.0)[None] * lg[:, None, None]), 0.0)
    k_decay = jnp.exp((C - 1 - idx)[None, :] * lg[:, None])
    q_decay = jnp.exp((idx + 1.0)[None, :] * lg[:, None])
    chunk_decay = jnp.exp(C * lg)
    qdec = jnp.broadcast_to(q_decay[:, :, None], (N_HEADS, C, HEAD_DIM))
    kdec = jnp.broadcast_to(k_decay[:, :, None], (N_HEADS, C, HEAD_DIM))
    cdec = jnp.broadcast_to(chunk_decay[:, None, None], (N_HEADS, 1, HEAD_DIM))

    const3 = lambda shape: pl.BlockSpec(shape, lambda b, i: (0, 0, 0))
    const2 = lambda shape: pl.BlockSpec(shape, lambda b, i: (0, 0))
    seq = lambda w: pl.BlockSpec((None, ts, w), lambda b, i: (b, i, 0))
    return pl.pallas_call(
        functools.partial(_even_kernel, ts=ts),
        grid=(B, S // ts),
        in_specs=[seq(EVEN_IN), seq(HEAD_DIM), seq(HEAD_DIM),
                  const3((N_HEADS, C, C)), const3((N_HEADS, C, HEAD_DIM)), const3((N_HEADS, C, HEAD_DIM)),
                  const3((N_HEADS, 1, HEAD_DIM)), const2((1, HALF_WIDTH)),
                  const3((len(POOL_WINDOWS), POOL_GROUP, POOL_GROUP)), const2((1, HALF_WIDTH))],
        out_specs=[seq(HALF_WIDTH), seq(HALF_WIDTH)],
        out_shape=[jax.ShapeDtypeStruct((B, S, HALF_WIDTH), BF16)] * 2,
        scratch_shapes=[pltpu.VMEM((N_HEADS, HEAD_DIM, HEAD_DIM), F32),
                        pltpu.VMEM((POOL_HALO + ts, HALF_WIDTH), F32),
                        pltpu.VMEM((POOL_HALO + ts, HALF_WIDTH), F32),
                        pltpu.VMEM((POOL_HALO + ts, 3 * POOL_GROUP), F32),
                        pltpu.VMEM((POOL_HALO + ts, 2 * POOL_GROUP), F32)],
        compiler_params=_params("arbitrary", "arbitrary"),
        name="even_core",
    )(z, cos, sin, decay, qdec, kdec, cdec, ret_norm_g[None, :], pool_w_bf16, pool_scale[None, :])


def _lru_kernel(z_ref, cw_ref, cb_ref, wa_ref, ba_ref, wx_ref, bx_ref, lam_ref, y_ref, ubuf_ref, hcar_ref, *, ts):
    sblk = pl.program_id(1)
    H = CONV_HALO

    @pl.when(sblk == 0)
    def _():
        ubuf_ref[0:H, :] = jnp.zeros((H, HALF_WIDTH), F32)
        hcar_ref[...] = jnp.zeros_like(hcar_ref)

    ubuf_ref[H:H + ts, :] = z_ref[:, pl.ds(HALF_WIDTH, HALF_WIDTH)]
    u = cb_ref[...] + cw_ref[CONV_WIDTH - 1:CONV_WIDTH, :] * ubuf_ref[H:H + ts, :]
    for k in range(CONV_WIDTH - 1):
        off = H - (CONV_WIDTH - 1) + k
        u = u + cw_ref[k:k + 1, :] * ubuf_ref[off:off + ts, :]
    ubuf_ref[0:H, :] = ubuf_ref[ts:ts + H, :]

    ub = u.astype(BF16)
    r = _sigmoid(_dot(ub, wa_ref[...]) + ba_ref[...])
    ig = _sigmoid(_dot(ub, wx_ref[...]) + bx_ref[...])
    nl = -lam_ref[...]
    softplus = jnp.maximum(nl, 0.0) + jnp.log1p(jnp.exp(-jnp.abs(nl)))
    log_a = (-LRU_C * r) * softplus
    a = jnp.exp(log_a)
    bseq = jnp.sqrt(jnp.tanh(-log_a) * (1.0 + a * a)) * (ig * u)

    row = lax.broadcasted_iota(jnp.int32, (ts, 1), 0)
    sh = 1
    while sh < ts:
        a_prev = pltpu.roll(a, sh, axis=0)
        b_prev = pltpu.roll(bseq, sh, axis=0)
        m = row >= sh
        bseq = jnp.where(m, a * b_prev + bseq, bseq)
        a = jnp.where(m, a * a_prev, a)
        sh *= 2
    h = bseq + a * hcar_ref[0:1, :]
    hcar_ref[...] = jnp.broadcast_to(h[ts - 1:ts, :], hcar_ref.shape)
    y_ref[...] = (h * _gelu_tanh(z_ref[:, 0:HALF_WIDTH])).astype(y_ref.dtype)


def _block_diag(w):
    n, c, d = w.shape
    eye = jnp.eye(n, dtype=w.dtype)
    return (w[:, :, None, :] * eye[:, None, :, None]).reshape(n * c, n * d)


def _lru_core(z, conv_w, conv_b, gate_a_w, gate_a_b, gate_x_w, gate_x_b, lam):
    B, S, _ = z.shape
    ts = 512
    W = HALF_WIDTH
    wa = _block_diag(gate_a_w).astype(BF16)
    wx = _block_diag(gate_x_w).astype(BF16)
    const = lambda shape: pl.BlockSpec(shape, lambda b, i: (0, 0))
    return pl.pallas_call(
        functools.partial(_lru_kernel, ts=ts),
        grid=(B, S // ts),
        in_specs=[pl.BlockSpec((None, ts, 2 * W), lambda b, i: (b, i, 0)),
                  const((CONV_WIDTH, W)), const((1, W)), const((W, W)), const((1, W)),
                  const((W, W)), const((1, W)), const((1, W))],
        out_specs=pl.BlockSpec((None, ts, W), lambda b, i: (b, i, 0)),
        out_shape=jax.ShapeDtypeStruct((B, S, W), BF16),
        scratch_shapes=[pltpu.VMEM((CONV_HALO + ts, W), F32), pltpu.VMEM((8, W), F32)],
        compiler_params=_params("arbitrary", "arbitrary"),
        name="lru_core",
    )(z, conv_w, conv_b[None, :], wa, gate_a_b[None, :], wx, gate_x_b[None, :], lam[None, :])


def _qkv_rm_kernel(x_ref, w_ref, cos_ref, sin_ref, q_ref, k_ref, v_ref, xs_ref):
    R = ATT_RES
    rows_per = x_ref.shape[0] // R
    ncol = x_ref.shape[1] // HEAD_DIM

    def residue_rows(ref, r):
        return ref[pl.ds(r, rows_per, stride=R), :]

    for c in range(ncol):
        xs_ref[c] = x_ref[:, c * HEAD_DIM:(c + 1) * HEAD_DIM]
    x = jnp.concatenate(
        [jnp.concatenate([residue_rows(xs_ref.at[c], r) for c in range(ncol)], axis=1) for r in range(R)],
        axis=0).astype(BF16)
    z = _dot(x, w_ref[...])
    qscale = HEAD_DIM ** -0.5 * math.log2(math.e)
    for r in range(R):
        rows = slice(r * rows_per, (r + 1) * rows_per)
        cos = residue_rows(cos_ref, r)
        sin = residue_rows(sin_ref, r)
        for h in range(N_HEADS):
            lanes = slice(h * HEAD_DIM, (h + 1) * HEAD_DIM)
            q_ref[r, :, lanes] = _rope(z[rows, h * HEAD_DIM:(h + 1) * HEAD_DIM], cos, sin) * qscale
            k_ref[r, :, lanes] = _rope(z[rows, HALF_WIDTH + h * HEAD_DIM:HALF_WIDTH + (h + 1) * HEAD_DIM], cos, sin)
        v_ref[r] = z[rows, 2 * HALF_WIDTH:3 * HALF_WIDTH]


def _qkv_rm(x3, w_bf16, cos, sin):
    B, S, D = x3.shape
    R = ATT_RES
    I = S // R
    W = HALF_WIDTH
    ts = 1024
    seq = lambda w: pl.BlockSpec((None, ts, w), lambda b, t: (b, t, 0))
    out_spec = pl.BlockSpec((None, R, ts // R, W), lambda b, t: (b, 0, t, 0))
    return pl.pallas_call(
        _qkv_rm_kernel,
        grid=(B, S // ts),
        in_specs=[seq(D), pl.BlockSpec((D, 3 * W), lambda b, t: (0, 0), pipeline_mode=pl.Buffered(1)),
                  seq(HEAD_DIM), seq(HEAD_DIM)],
        out_specs=[out_spec] * 3,
        out_shape=[jax.ShapeDtypeStruct((B, R, I, W), F32)] * 3,
        scratch_shapes=[pltpu.VMEM((D // HEAD_DIM, ts, HEAD_DIM), F32)],
        compiler_params=_params("arbitrary", "arbitrary"),
        name="qkv_rm",
    )(x3, w_bf16, cos, sin)


def _pattern_bias(dil):
    nslab = ATT_RES // dil
    slab = ATT_BLOCK // nslab
    row = np.arange(ATT_BLOCK)
    col = np.arange(2 * ATT_BLOCK)
    jq = nslab * (row % slab) + row // slab + ATT_BLOCK
    jk = nslab * (col % slab) + (col % ATT_BLOCK) // slab + ATT_BLOCK * (col // ATT_BLOCK)
    delta = jq[:, None] - jk[None, :]
    return np.where((delta >= 0) & (delta <= ATT_BLOCK), 0.0, -np.inf).astype(np.float32)


def _att_rm_kernel(q_ref, k_ref, v_ref, bias_ref, o_ref, num_ref, den_ref, mx_ref, onat_ref):
    QB = ATT_BLOCK
    TI = ATT_TILE // ATT_RES
    ones = jnp.ones((2 * QB, HEAD_DIM), BF16)

    def gather(ref, slabs, r0, rn):
        return jnp.concatenate([ref[r, r0:r0 + rn, :] for r in slabs], axis=0)

    for tile in range(q_ref.shape[1] // TI):
        i0 = tile * TI
        for p, (window, dil) in enumerate(DIL_PATTERNS):
            assert window // dil == QB
            nslab = ATT_RES // dil
            slab = QB // nslab
            for n in range(ATT_TILE // QB):
                res, m = n % dil, n // dil
                slabs = [res + dil * a for a in range(nslab)]
                c0 = i0 + slab * m
                qb = gather(q_ref, slabs, c0, slab).astype(BF16)
                if c0 == 0:
                    kb = gather(k_ref, slabs, c0, slab).astype(BF16)
                    vb = gather(v_ref, slabs, c0, slab).astype(BF16)
                    bias = bias_ref[p, :, QB:2 * QB]
                else:
                    kb = jnp.concatenate([gather(k_ref, slabs, c0 - slab, slab),
                                          gather(k_ref, slabs, c0, slab)], axis=0).astype(BF16)
                    vb = jnp.concatenate([gather(v_ref, slabs, c0 - slab, slab),
                                          gather(v_ref, slabs, c0, slab)], axis=0).astype(BF16)
                    bias = bias_ref[p]
                s = _dot_nt(qb, kb) + bias
                mx = jnp.max(s, axis=-1, keepdims=True)
                e = jnp.exp2(s - mx).astype(BF16)
                nd = _dot(e, jnp.concatenate([vb, ones[0:vb.shape[0]]], axis=1))
                mxb = jnp.broadcast_to(mx, (QB, HEAD_DIM))
                for a, r in enumerate(slabs):
                    rows = slice(a * slab, (a + 1) * slab)
                    dst = slice(slab * m, slab * (m + 1))
                    num_ref[p, r, dst, :] = nd[rows, 0:HEAD_DIM]
                    den_ref[p, r, dst, :] = nd[rows, HEAD_DIM:2 * HEAD_DIM]
                    mx_ref[p, r, dst, :] = mxb[rows]

        m0, m1, m2 = mx_ref[0], mx_ref[1], mx_ref[2]
        mm = jnp.maximum(jnp.maximum(m0, m1), m2)
        w0, w1, w2 = jnp.exp2(m0 - mm), jnp.exp2(m1 - mm), jnp.exp2(m2 - mm)
        num = w0 * num_ref[0] + w1 * num_ref[1] + w2 * num_ref[2]
        den = w0 * den_ref[0] + w1 * den_ref[1] + w2 * den_ref[2]
        out = num / den
        for r in range(ATT_RES):
            onat_ref[pl.ds(r, TI, stride=ATT_RES), :] = out[r]
        o_ref[tile * ATT_TILE:(tile + 1) * ATT_TILE, :] = onat_ref[...].astype(o_ref.dtype)


def _att_rm(q, k, v):
    B, R, I, W = q.shape
    TI = ATT_TILE // ATT_RES
    assert R == ATT_RES and I % TI == 0
    bias = jnp.asarray(np.stack([_pattern_bias(dil) for _, dil in DIL_PATTERNS]))
    blk = pl.BlockSpec((None, R, I, HEAD_DIM), lambda b, h: (b, 0, 0, h))
    acc = pltpu.VMEM((len(DIL_PATTERNS), R, TI, HEAD_DIM), F32)
    return pl.pallas_call(
        _att_rm_kernel,
        grid=(B, N_HEADS),
        in_specs=[blk, blk, blk,
                  pl.BlockSpec(bias.shape, lambda b, h: (0, 0, 0), pipeline_mode=pl.Buffered(1))],
        out_specs=pl.BlockSpec((None, R * I, HEAD_DIM), lambda b, h: (b, 0, h)),
        out_shape=jax.ShapeDtypeStruct((B, R * I, W), BF16),
        scratch_shapes=[acc, acc, acc, pltpu.VMEM((ATT_TILE, HEAD_DIM), F32)],
        compiler_params=_params("arbitrary", "arbitrary"),
        name="att_rm",
    )(q, k, v, bias)


def _out_ln_kernel(x_ref, a1_ref, a2_ref, w_ref, g_ref, b_ref, o_ref):
    W = HALF_WIDTH
    mix = _dot(a1_ref[...], w_ref[0:W, :]) + _dot(a2_ref[...], w_ref[W:2 * W, :])
    o_ref[...] = _layer_norm(DEEPNORM_ALPHA * x_ref[...] + mix, g_ref[...], b_ref[...])


def _out_ln(x2d, a1, a2, w_bf16, g, b):
    T, D = x2d.shape
    tm = 512
    rowblk = lambda w: pl.BlockSpec((tm, w), lambda i: (i, 0))
    const = lambda shape: pl.BlockSpec(shape, lambda i: (0, 0))
    return pl.pallas_call(
        _out_ln_kernel,
        grid=(T // tm,),
        in_specs=[rowblk(D), rowblk(HALF_WIDTH), rowblk(HALF_WIDTH), const((D, D)), const((1, D)), const((1, D))],
        out_specs=rowblk(D),
        out_shape=jax.ShapeDtypeStruct((T, D), F32),
        compiler_params=_params("arbitrary"),
        name="out_ln",
    )(x2d, a1, a2, w_bf16, g[None, :], b[None, :])


FFN_SUB = 512
FFN_CHUNK = 256


def _ffn_kernel(x_ref, wi_ref, wo_ref, g_ref, b_ref, o_ref, *, tm):
    for r0 in range(0, tm, FFN_SUB):
        rows = slice(r0, r0 + FFN_SUB)
        xb = x_ref[rows, :].astype(BF16)
        acc = None
        for c0 in range(0, D_FF, FFN_CHUNK):
            gate = _dot(xb, wi_ref[:, c0:c0 + FFN_CHUNK])
            up = _dot(xb, wi_ref[:, D_FF + c0:D_FF + c0 + FFN_CHUNK])
            part = _dot((_silu(gate) * up).astype(BF16), wo_ref[c0:c0 + FFN_CHUNK, :])
            acc = part if acc is None else acc + part
        o_ref[rows, :] = _layer_norm(DEEPNORM_ALPHA * x_ref[rows, :] + acc, g_ref[...], b_ref[...])


def _ffn_ln(x2d, w_in_bf16, w_out_bf16, g, b):
    T, D = x2d.shape
    tm = 1024
    resident = lambda shape: pl.BlockSpec(shape, lambda i: (0, 0), pipeline_mode=pl.Buffered(1))
    return pl.pallas_call(
        functools.partial(_ffn_kernel, tm=tm),
        grid=(T // tm,),
        in_specs=[pl.BlockSpec((tm, D), lambda i: (i, 0)),
                  resident((D, 2 * D_FF)), resident((D_FF, D)), resident((1, D)), resident((1, D))],
        out_specs=pl.BlockSpec((tm, D), lambda i: (i, 0)),
        out_shape=jax.ShapeDtypeStruct((T, D), F32),
        compiler_params=_params("arbitrary"),
        name="ffn_ln",
    )(x2d, w_in_bf16, w_out_bf16, g[None, :], b[None, :])


def kernel(x, positions, ev_w_in, ev_ret_norm_g, ev_pool_w, ev_pool_scale, ev_w_out, od_w_in, od_conv_w, od_conv_b, od_gate_a_w, od_gate_a_b, od_gate_x_w, od_gate_x_b, od_lru_lambda, od_w_out, ffn_w_in, ffn_w_out, ln_g, ln_b):
    B, S, D = x.shape
    T = B * S
    cos, sin = _rope_tables(positions)
    h = x.reshape(T, D)
    for layer in range(DEPTH):
        j = layer // 2
        if layer % 2 == 0:
            z = _in_proj(h, ev_w_in[j].astype(BF16)).reshape(B, S, EVEN_IN)
            a1, a2 = _even_core(z, cos, sin, ev_ret_norm_g[j], ev_pool_w[j].astype(BF16), ev_pool_scale[j])
            w_out = ev_w_out[j]
        else:
            w_in = od_w_in[j].astype(BF16)
            z = _in_proj(h, w_in[:, :2 * HALF_WIDTH]).reshape(B, S, 2 * HALF_WIDTH)
            a1 = _lru_core(z, od_conv_w[j], od_conv_b[j], od_gate_a_w[j], od_gate_a_b[j],
                           od_gate_x_w[j], od_gate_x_b[j], od_lru_lambda[j])
            a2 = _att_rm(*_qkv_rm(h.reshape(B, S, D), w_in[:, 2 * HALF_WIDTH:], cos, sin))
            w_out = od_w_out[j]
        h = _out_ln(h, a1.reshape(T, HALF_WIDTH), a2.reshape(T, HALF_WIDTH), w_out.astype(BF16),
                    ln_g[layer, 0], ln_b[layer, 0])
        h = _ffn_ln(h, ffn_w_in[layer].astype(BF16), ffn_w_out[layer].astype(BF16), ln_g[layer, 1], ln_b[layer, 1])
    return h.reshape(B, S, D)
```

```python
import functools
import math

import jax
import jax.numpy as jnp
import numpy as np
from jax import lax
from jax.experimental import pallas as pl
from jax.experimental.pallas import tpu as pltpu

F32 = jnp.float32
BF16 = jnp.bfloat16

D_MODEL = 1024
DEPTH = 4
HEAD_DIM = 128
HALF_WIDTH = 512
N_HEADS = HALF_WIDTH // HEAD_DIM
RET_CHUNK = 128
POOL_WINDOWS = (2, 4, 8, 16)
POOL_GROUP = 128
POOL_HALO = 32
LRU_BLOCKS = 8
LRU_C = 8.0
CONV_WIDTH = 4
CONV_HALO = 8
DIL_PATTERNS = ((128, 1), (512, 4), (2048, 16))
ATT_BLOCK = 128
ATT_TILE = 2048
ATT_RES = 16
ROPE_THETA = 10000.0
EVEN_IN = 5 * HALF_WIDTH
ODD_IN = 5 * HALF_WIDTH
D_FF = 2816
DEEPNORM_ALPHA = (2 * DEPTH) ** 0.25
LN_EPS = 1e-5

VMEM_LIMIT = 56 * 1024 * 1024


def _params(*sem):
    return pltpu.CompilerParams(dimension_semantics=sem, vmem_limit_bytes=VMEM_LIMIT)


def _sigmoid(x):
    return 1.0 / (1.0 + jnp.exp(-x))


def _silu(x):
    return x * _sigmoid(x)


def _gelu_tanh(x):
    c = math.sqrt(2.0 / math.pi)
    return x * (0.5 * (1.0 + jnp.tanh(c * (x + 0.044715 * (x * x * x)))))


def _layer_norm(y, g, b):
    mu = jnp.mean(y, axis=-1, keepdims=True)
    d = y - mu
    var = jnp.mean(d * d, axis=-1, keepdims=True)
    return d * lax.rsqrt(var + LN_EPS) * g + b


def _rope(t, cos, sin_signed):
    return t * cos + pltpu.roll(t, HEAD_DIM // 2, axis=1) * sin_signed


def _dot(a, b):
    return jnp.dot(a, b, preferred_element_type=F32)


def _dot_nt(a, b):
    return lax.dot_general(a, b, (((1,), (1,)), ((), ())), preferred_element_type=F32)


def _rope_table_kernel(pos_ref, inv_ref, sign_ref, cos_ref, sin_ref):
    ang = pos_ref[...].astype(F32) * inv_ref[...]
    cos_ref[...] = jnp.cos(ang)
    sin_ref[...] = jnp.sin(ang) * sign_ref[...]


def _rope_tables(positions):
    B, S = positions.shape
    T = B * S
    ts = 2048
    half = HEAD_DIM // 2
    inv = ROPE_THETA ** (-jnp.arange(0, HEAD_DIM, 2, dtype=F32) / HEAD_DIM)
    inv2 = jnp.concatenate([inv, inv])[None, :]
    sign = jnp.concatenate([-jnp.ones((half,), F32), jnp.ones((half,), F32)])[None, :]
    row = pl.BlockSpec((1, HEAD_DIM), lambda i: (0, 0))
    cos, sin = pl.pallas_call(
        _rope_table_kernel,
        grid=(T // ts,),
        in_specs=[pl.BlockSpec((ts, 1), lambda i: (i, 0)), row, row],
        out_specs=[pl.BlockSpec((ts, HEAD_DIM), lambda i: (i, 0))] * 2,
        out_shape=[jax.ShapeDtypeStruct((T, HEAD_DIM), F32)] * 2,
        compiler_params=_params("arbitrary"),
        name="rope_tables",
    )(positions.reshape(T, 1), inv2, sign)
    return cos.reshape(B, S, HEAD_DIM), sin.reshape(B, S, HEAD_DIM)


def _even_kernel(x_ref, wi_ref, cos_ref, sin_ref, decay_ref, qdec_ref, kdec_ref, cdec_ref, gn_ref, pw_ref, ps_ref,
                 ret_ref, pool_ref, z_ref, state_ref, pbuf_ref, s2_ref, s4_ref, s8_ref, *, ts):
    sblk = pl.program_id(1)
    H = POOL_HALO
    z_ref[...] = _dot(x_ref[...].astype(BF16), wi_ref[...])

    @pl.when(sblk == 0)
    def _():
        state_ref[...] = jnp.zeros_like(state_ref)
        pbuf_ref[0:H, :] = jnp.zeros((H, HALF_WIDTH), F32)

    for c in range(ts // RET_CHUNK):
        rows = pl.ds(c * RET_CHUNK, RET_CHUNK)
        cos = cos_ref[rows, :]
        sin = sin_ref[rows, :]
        for h in range(N_HEADS):
            lanes = slice(h * HEAD_DIM, (h + 1) * HEAD_DIM)
            q = _rope(z_ref[rows, lanes], cos, sin)
            k = _rope(z_ref[rows, pl.ds(HALF_WIDTH + h * HEAD_DIM, HEAD_DIM)], cos, sin) * (HEAD_DIM ** -0.5)
            v = z_ref[rows, pl.ds(2 * HALF_WIDTH + h * HEAD_DIM, HEAD_DIM)]
            gate = z_ref[rows, pl.ds(3 * HALF_WIDTH + h * HEAD_DIM, HEAD_DIM)]
            qb = q.astype(BF16)
            kb = k.astype(BF16)
            vb = v.astype(BF16)
            scores = _dot_nt(qb, kb) * decay_ref[h]
            inner = _dot(scores.astype(BF16), vb)
            state = state_ref[h]
            cross = _dot(qb, state.astype(BF16)) * qdec_ref[h]
            kd = (k * kdec_ref[h]).T.astype(BF16)
            state_ref[h] = state * cdec_ref[h] + _dot(kd, vb)
            y = inner + cross
            mu = jnp.mean(y, axis=-1, keepdims=True)
            d = y - mu
            var = jnp.mean(d * d, axis=-1, keepdims=True)
            yn = d * lax.rsqrt(var + LN_EPS) * gn_ref[:, lanes]
            ret_ref[rows, lanes] = (yn * _silu(gate)).astype(ret_ref.dtype)

    pbuf_ref[H:H + ts, :] = z_ref[:, pl.ds(4 * HALF_WIDTH, HALF_WIDTH)]
    G = POOL_GROUP
    s2_ref[8:H + ts, :] = pbuf_ref[8:H + ts, :] + pbuf_ref[7:H + ts - 1, :]
    s4_ref[16:H + ts, :] = s2_ref[16:H + ts, G:4 * G] + s2_ref[14:H + ts - 2, G:4 * G]
    s8_ref[24:H + ts, :] = s4_ref[24:H + ts, G:3 * G] + s4_ref[20:H + ts - 4, G:3 * G]
    s16 = s8_ref[32:H + ts, G:2 * G] + s8_ref[24:H + ts - 8, G:2 * G]
    wsums = (s2_ref[H:H + ts, 0:G], s4_ref[H:H + ts, 0:G], s8_ref[H:H + ts, 0:G], s16)
    tpos = sblk * ts + lax.broadcasted_iota(jnp.int32, (ts, 1), 0) + 1
    for gi, w in enumerate(POOL_WINDOWS):
        lanes = slice(gi * G, (gi + 1) * G)
        cnt = jnp.minimum(tpos, w).astype(F32)
        pooled = wsums[gi] / cnt - pbuf_ref[H:H + ts, lanes]
        mixed = _dot(pooled.astype(BF16), pw_ref[gi]) * ps_ref[:, lanes]
        pool_ref[:, lanes] = mixed.astype(pool_ref.dtype)
    pbuf_ref[0:H, :] = pbuf_ref[ts:ts + H, :]


def _even_core(x3, w_in_bf16, cos, sin, ret_norm_g, pool_w_bf16, pool_scale):
    B, S, D = x3.shape
    ts = 512
    C = RET_CHUNK
    lg = jnp.log1p(-(2.0 ** (-5.0 - jnp.arange(N_HEADS, dtype=F32))))
    idx = jnp.arange(C, dtype=F32)
    rel = idx[:, None] - idx[None, :]
    decay = jnp.where(rel[None] >= 0, jnp.exp(jnp.maximum(rel, 0.0)[None] * lg[:, None, None]), 0.0)
    k_decay = jnp.exp((C - 1 - idx)[None, :] * lg[:, None])
    q_decay = jnp.exp((idx + 1.0)[None, :] * lg[:, None])
    chunk_decay = jnp.exp(C * lg)
    qdec = jnp.broadcast_to(q_decay[:, :, None], (N_HEADS, C, HEAD_DIM))
    kdec = jnp.broadcast_to(k_decay[:, :, None], (N_HEADS, C, HEAD_DIM))
    cdec = jnp.broadcast_to(chunk_decay[:, None, None], (N_HEADS, 1, HEAD_DIM))

    const3 = lambda shape: pl.BlockSpec(shape, lambda b, i: (0, 0, 0), pipeline_mode=pl.Buffered(1))
    const2 = lambda shape: pl.BlockSpec(shape, lambda b, i: (0, 0), pipeline_mode=pl.Buffered(1))
    seq = lambda w: pl.BlockSpec((None, ts, w), lambda b, i: (b, i, 0))
    return pl.pallas_call(
        functools.partial(_even_kernel, ts=ts),
        grid=(B, S // ts),
        in_specs=[seq(D), const2((D, EVEN_IN)), seq(HEAD_DIM), seq(HEAD_DIM),
                  const3((N_HEADS, C, C)), const3((N_HEADS, C, HEAD_DIM)), const3((N_HEADS, C, HEAD_DIM)),
                  const3((N_HEADS, 1, HEAD_DIM)), const2((1, HALF_WIDTH)),
                  const3((len(POOL_WINDOWS), POOL_GROUP, POOL_GROUP)), const2((1, HALF_WIDTH))],
        out_specs=[seq(HALF_WIDTH), seq(HALF_WIDTH)],
        out_shape=[jax.ShapeDtypeStruct((B, S, HALF_WIDTH), BF16)] * 2,
        scratch_shapes=[pltpu.VMEM((ts, EVEN_IN), F32),
                        pltpu.VMEM((N_HEADS, HEAD_DIM, HEAD_DIM), F32),
                        pltpu.VMEM((POOL_HALO + ts, HALF_WIDTH), F32),
                        pltpu.VMEM((POOL_HALO + ts, HALF_WIDTH), F32),
                        pltpu.VMEM((POOL_HALO + ts, 3 * POOL_GROUP), F32),
                        pltpu.VMEM((POOL_HALO + ts, 2 * POOL_GROUP), F32)],
        compiler_params=_params("arbitrary", "arbitrary"),
        name="even_core",
    )(x3, w_in_bf16, cos, sin, decay, qdec, kdec, cdec, ret_norm_g[None, :], pool_w_bf16, pool_scale[None, :])


def _lru_kernel(x_ref, wi_ref, cw_ref, cb_ref, wa_ref, ba_ref, wx_ref, bx_ref, lam_ref, y_ref,
                z_ref, ubuf_ref, hcar_ref, h_ref, *, ts):
    sblk = pl.program_id(1)
    H = CONV_HALO

    @pl.when(sblk == 0)
    def _():
        ubuf_ref[0:H, :] = jnp.zeros((H, HALF_WIDTH), F32)
        hcar_ref[...] = jnp.zeros_like(hcar_ref)

    z_ref[...] = _dot(x_ref[...].astype(BF16), wi_ref[...])
    ubuf_ref[H:H + ts, :] = z_ref[:, pl.ds(HALF_WIDTH, HALF_WIDTH)]
    u = cb_ref[...] + cw_ref[CONV_WIDTH - 1:CONV_WIDTH, :] * ubuf_ref[H:H + ts, :]
    for k in range(CONV_WIDTH - 1):
        off = H - (CONV_WIDTH - 1) + k
        u = u + cw_ref[k:k + 1, :] * ubuf_ref[off:off + ts, :]
    ubuf_ref[0:H, :] = ubuf_ref[ts:ts + H, :]

    ub = u.astype(BF16)
    r = _sigmoid(_dot(ub, wa_ref[...]) + ba_ref[...])
    ig = _sigmoid(_dot(ub, wx_ref[...]) + bx_ref[...])
    nl = -lam_ref[...]
    softplus = jnp.maximum(nl, 0.0) + jnp.log1p(jnp.exp(-jnp.abs(nl)))
    log_a = (-LRU_C * r) * softplus
    a = jnp.exp(log_a)
    bseq = jnp.sqrt(jnp.tanh(-log_a) * (1.0 + a * a)) * (ig * u)

    G = 8
    a = a.reshape(ts // G, G, HALF_WIDTH)
    bseq = bseq.reshape(ts // G, G, HALF_WIDTH)
    row = lax.broadcasted_iota(jnp.int32, (1, G, 1), 1)
    sh = 1
    while sh < G:
        a_prev = pltpu.roll(a, sh, axis=1)
        b_prev = pltpu.roll(bseq, sh, axis=1)
        m = row >= sh
        bseq = jnp.where(m, a * b_prev + bseq, bseq)
        a = jnp.where(m, a * a_prev, a)
        sh *= 2
    carry = hcar_ref[0:1, :]
    for v in range(ts // G):
        hv = bseq[v] + a[v] * carry
        h_ref[G * v:G * (v + 1), :] = hv
        carry = hv[G - 1:G]
    hcar_ref[...] = jnp.broadcast_to(carry, hcar_ref.shape)
    y_ref[...] = (h_ref[...] * _gelu_tanh(z_ref[:, 0:HALF_WIDTH])).astype(y_ref.dtype)


def _block_diag(w):
    n, c, d = w.shape
    eye = jnp.eye(n, dtype=w.dtype)
    return (w[:, :, None, :] * eye[:, None, :, None]).reshape(n * c, n * d)


def _lru_core(x3, w_in_bf16, conv_w, conv_b, gate_a_w, gate_a_b, gate_x_w, gate_x_b, lam):
    B, S, D = x3.shape
    ts = 512
    W = HALF_WIDTH
    wa = _block_diag(gate_a_w).astype(BF16)
    wx = _block_diag(gate_x_w).astype(BF16)
    const = lambda shape: pl.BlockSpec(shape, lambda b, i: (0, 0), pipeline_mode=pl.Buffered(1))
    return pl.pallas_call(
        functools.partial(_lru_kernel, ts=ts),
        grid=(B, S // ts),
        in_specs=[pl.BlockSpec((None, ts, D), lambda b, i: (b, i, 0)), const((D, 2 * W)),
                  const((CONV_WIDTH, W)), const((1, W)), const((W, W)), const((1, W)),
                  const((W, W)), const((1, W)), const((1, W))],
        out_specs=pl.BlockSpec((None, ts, W), lambda b, i: (b, i, 0)),
        out_shape=jax.ShapeDtypeStruct((B, S, W), BF16),
        scratch_shapes=[pltpu.VMEM((ts, 2 * W), F32), pltpu.VMEM((CONV_HALO + ts, W), F32),
                        pltpu.VMEM((8, W), F32), pltpu.VMEM((ts, W), F32)],
        compiler_params=_params("arbitrary", "arbitrary"),
        name="lru_core",
    )(x3, w_in_bf16, conv_w, conv_b[None, :], wa, gate_a_b[None, :], wx, gate_x_b[None, :], lam[None, :])


def _qkv_rm_kernel(x_ref, w_ref, cos_ref, sin_ref, q_ref, k_ref, v_ref, xs_ref):
    R = ATT_RES
    rows_per = x_ref.shape[0] // R
    ncol = x_ref.shape[1] // HEAD_DIM

    def residue_rows(ref, r):
        return ref[pl.ds(r, rows_per, stride=R), :]

    for c in range(ncol):
        xs_ref[c] = x_ref[:, c * HEAD_DIM:(c + 1) * HEAD_DIM]
    x = jnp.concatenate(
        [jnp.concatenate([residue_rows(xs_ref.at[c], r) for c in range(ncol)], axis=1) for r in range(R)],
        axis=0).astype(BF16)
    z = _dot(x, w_ref[...])
    qscale = HEAD_DIM ** -0.5 * math.log2(math.e)
    for r in range(R):
        rows = slice(r * rows_per, (r + 1) * rows_per)
        cos = residue_rows(cos_ref, r)
        sin = residue_rows(sin_ref, r)
        for h in range(N_HEADS):
            lanes = slice(h * HEAD_DIM, (h + 1) * HEAD_DIM)
            q_ref[r, :, lanes] = _rope(z[rows, h * HEAD_DIM:(h + 1) * HEAD_DIM], cos, sin) * qscale
            k_ref[r, :, lanes] = _rope(z[rows, HALF_WIDTH + h * HEAD_DIM:HALF_WIDTH + (h + 1) * HEAD_DIM], cos, sin)
        v_ref[r] = z[rows, 2 * HALF_WIDTH:3 * HALF_WIDTH]


def _qkv_rm(x3, w_bf16, cos, sin):
    B, S, D = x3.shape
    R = ATT_RES
    I = S // R
    W = HALF_WIDTH
    ts = 1024
    seq = lambda w: pl.BlockSpec((None, ts, w), lambda b, t: (b, t, 0))
    out_spec = pl.BlockSpec((None, R, ts // R, W), lambda b, t: (b, 0, t, 0))
    return pl.pallas_call(
        _qkv_rm_kernel,
        grid=(B, S // ts),
        in_specs=[seq(D), pl.BlockSpec((D, 3 * W), lambda b, t: (0, 0), pipeline_mode=pl.Buffered(1)),
                  seq(HEAD_DIM), seq(HEAD_DIM)],
        out_specs=[out_spec] * 3,
        out_shape=[jax.ShapeDtypeStruct((B, R, I, W), F32)] * 3,
        scratch_shapes=[pltpu.VMEM((D // HEAD_DIM, ts, HEAD_DIM), F32)],
        compiler_params=_params("arbitrary", "arbitrary"),
        name="qkv_rm",
    )(x3, w_bf16, cos, sin)


def _pattern_bias(dil):
    nslab = ATT_RES // dil
    slab = ATT_BLOCK // nslab
    row = np.arange(ATT_BLOCK)
    col = np.arange(2 * ATT_BLOCK)
    jq = nslab * (row % slab) + row // slab + ATT_BLOCK
    jk = nslab * (col % slab) + (col % ATT_BLOCK) // slab + ATT_BLOCK * (col // ATT_BLOCK)
    delta = jq[:, None] - jk[None, :]
    return np.where((delta >= 0) & (delta <= ATT_BLOCK), 0.0, -np.inf).astype(np.float32)


def _att_rm_kernel(q_ref, k_ref, v_ref, bias_ref, o_ref, num_ref, den_ref, mx_ref, onat_ref):
    QB = ATT_BLOCK
    TI = ATT_TILE // ATT_RES
    ones = jnp.ones((2 * QB, HEAD_DIM), BF16)

    def gather(ref, slabs, r0, rn):
        return jnp.concatenate([ref[r, r0:r0 + rn, :] for r in slabs], axis=0)

    for tile in range(q_ref.shape[1] // TI):
        i0 = tile * TI
        for p, (window, dil) in enumerate(DIL_PATTERNS):
            assert window // dil == QB
            nslab = ATT_RES // dil
            slab = QB // nslab
            for n in range(ATT_TILE // QB):
                res, m = n % dil, n // dil
                slabs = [res + dil * a for a in range(nslab)]
                c0 = i0 + slab * m
                qb = gather(q_ref, slabs, c0, slab).astype(BF16)
                if c0 == 0:
                    kb = gather(k_ref, slabs, c0, slab).astype(BF16)
                    vb = gather(v_ref, slabs, c0, slab).astype(BF16)
                    bias = bias_ref[p, :, QB:2 * QB]
                else:
                    kb = jnp.concatenate([gather(k_ref, slabs, c0 - slab, slab),
                                          gather(k_ref, slabs, c0, slab)], axis=0).astype(BF16)
                    vb = jnp.concatenate([gather(v_ref, slabs, c0 - slab, slab),
                                          gather(v_ref, slabs, c0, slab)], axis=0).astype(BF16)
                    bias = bias_ref[p]
                s = _dot_nt(qb, kb) + bias
                mx = jnp.max(s, axis=-1, keepdims=True)
                e = jnp.exp2(s - mx).astype(BF16)
                nd = _dot(e, jnp.concatenate([vb, ones[0:vb.shape[0]]], axis=1))
                mxb = jnp.broadcast_to(mx, (QB, HEAD_DIM))
                for a, r in enumerate(slabs):
                    rows = slice(a * slab, (a + 1) * slab)
                    dst = slice(slab * m, slab * (m + 1))
                    num_ref[p, r, dst, :] = nd[rows, 0:HEAD_DIM]
                    den_ref[p, r, dst, :] = nd[rows, HEAD_DIM:2 * HEAD_DIM]
                    mx_ref[p, r, dst, :] = mxb[rows]

        m0, m1, m2 = mx_ref[0], mx_ref[1], mx_ref[2]
        mm = jnp.maximum(jnp.maximum(m0, m1), m2)
        w0, w1, w2 = jnp.exp2(m0 - mm), jnp.exp2(m1 - mm), jnp.exp2(m2 - mm)
        num = w0 * num_ref[0] + w1 * num_ref[1] + w2 * num_ref[2]
        den = w0 * den_ref[0] + w1 * den_ref[1] + w2 * den_ref[2]
        out = num / den
        for r in range(ATT_RES):
            onat_ref[pl.ds(r, TI, stride=ATT_RES), :] = out[r]
        o_ref[tile * ATT_TILE:(tile + 1) * ATT_TILE, :] = onat_ref[...].astype(o_ref.dtype)


def _att_rm(q, k, v):
    B, R, I, W = q.shape
    TI = ATT_TILE // ATT_RES
    assert R == ATT_RES and I % TI == 0
    bias = jnp.asarray(np.stack([_pattern_bias(dil) for _, dil in DIL_PATTERNS]))
    blk = pl.BlockSpec((None, R, I, HEAD_DIM), lambda b, h: (b, 0, 0, h))
    acc = pltpu.VMEM((len(DIL_PATTERNS), R, TI, HEAD_DIM), F32)
    return pl.pallas_call(
        _att_rm_kernel,
        grid=(B, N_HEADS),
        in_specs=[blk, blk, blk,
                  pl.BlockSpec(bias.shape, lambda b, h: (0, 0, 0), pipeline_mode=pl.Buffered(1))],
        out_specs=pl.BlockSpec((None, R * I, HEAD_DIM), lambda b, h: (b, 0, h)),
        out_shape=jax.ShapeDtypeStruct((B, R * I, W), BF16),
        scratch_shapes=[acc, acc, acc, pltpu.VMEM((ATT_TILE, HEAD_DIM), F32)],
        compiler_params=_params("arbitrary", "arbitrary"),
        name="att_rm",
    )(q, k, v, bias)


FFN_SUB = 512
FFN_CHUNK = 256


def _mix_ffn_kernel(x_ref, a1_ref, a2_ref, wm_ref, wi_ref, wo_ref, g_ref, b_ref, o_ref, *, tm):
    W = HALF_WIDTH
    for r0 in range(0, tm, FFN_SUB):
        rows = slice(r0, r0 + FFN_SUB)
        mix = _dot(a1_ref[rows, :], wm_ref[0:W, :]) + _dot(a2_ref[rows, :], wm_ref[W:2 * W, :])
        h1 = _layer_norm(DEEPNORM_ALPHA * x_ref[rows, :] + mix, g_ref[0:1, :], b_ref[0:1, :])
        hb = h1.astype(BF16)
        acc = None
        for c0 in range(0, D_FF, FFN_CHUNK):
            gate = _dot(hb, wi_ref[:, c0:c0 + FFN_CHUNK])
            up = _dot(hb, wi_ref[:, D_FF + c0:D_FF + c0 + FFN_CHUNK])
            part = _dot((_silu(gate) * up).astype(BF16), wo_ref[c0:c0 + FFN_CHUNK, :])
            acc = part if acc is None else acc + part
        o_ref[rows, :] = _layer_norm(DEEPNORM_ALPHA * h1 + acc, g_ref[1:2, :], b_ref[1:2, :])


def _mix_ffn(x2d, a1, a2, w_mix_bf16, w_in_bf16, w_out_bf16, g2, b2):
    T, D = x2d.shape
    tm = 1024
    rowblk = lambda w: pl.BlockSpec((tm, w), lambda i: (i, 0))
    resident = lambda shape: pl.BlockSpec(shape, lambda i: (0, 0), pipeline_mode=pl.Buffered(1))
    return pl.pallas_call(
        functools.partial(_mix_ffn_kernel, tm=tm),
        grid=(T // tm,),
        in_specs=[rowblk(D), rowblk(HALF_WIDTH), rowblk(HALF_WIDTH), resident((D, D)),
                  resident((D, 2 * D_FF)), resident((D_FF, D)), resident((2, D)), resident((2, D))],
        out_specs=rowblk(D),
        out_shape=jax.ShapeDtypeStruct((T, D), F32),
        compiler_params=_params("arbitrary"),
        name="mix_ffn",
    )(x2d, a1, a2, w_mix_bf16, w_in_bf16, w_out_bf16, g2, b2)


def kernel(x, positions, ev_w_in, ev_ret_norm_g, ev_pool_w, ev_pool_scale, ev_w_out, od_w_in, od_conv_w, od_conv_b, od_gate_a_w, od_gate_a_b, od_gate_x_w, od_gate_x_b, od_lru_lambda, od_w_out, ffn_w_in, ffn_w_out, ln_g, ln_b):
    B, S, D = x.shape
    T = B * S
    cos, sin = _rope_tables(positions)
    h = x.reshape(T, D)
    for layer in range(DEPTH):
        j = layer // 2
        h3 = h.reshape(B, S, D)
        if layer % 2 == 0:
            a1, a2 = _even_core(h3, ev_w_in[j].astype(BF16), cos, sin, ev_ret_norm_g[j],
                                ev_pool_w[j].astype(BF16), ev_pool_scale[j])
            w_mix = ev_w_out[j]
        else:
            w_in = od_w_in[j].astype(BF16)
            a1 = _lru_core(h3, w_in[:, :2 * HALF_WIDTH], od_conv_w[j], od_conv_b[j], od_gate_a_w[j], od_gate_a_b[j],
                           od_gate_x_w[j], od_gate_x_b[j], od_lru_lambda[j])
            a2 = _att_rm(*_qkv_rm(h3, w_in[:, 2 * HALF_WIDTH:], cos, sin))
            w_mix = od_w_out[j]
        h = _mix_ffn(h, a1.reshape(T, HALF_WIDTH), a2.reshape(T, HALF_WIDTH), w_mix.astype(BF16),
                     ffn_w_in[layer].astype(BF16), ffn_w_out[layer].astype(BF16), ln_g[layer], ln_b[layer])
    return h.reshape(B, S, D)
```

```python
import functools
import math

import jax
import jax.numpy as jnp
import numpy as np
from jax import lax
from jax.experimental import pallas as pl
from jax.experimental.pallas import tpu as pltpu

F32 = jnp.float32
BF16 = jnp.bfloat16

D_MODEL = 1024
DEPTH = 4
HEAD_DIM = 128
HALF_WIDTH = 512
N_HEADS = HALF_WIDTH // HEAD_DIM
RET_CHUNK = 128
POOL_WINDOWS = (2, 4, 8, 16)
POOL_GROUP = 128
POOL_HALO = 32
LRU_BLOCKS = 8
LRU_C = 8.0
CONV_WIDTH = 4
CONV_HALO = 8
DIL_PATTERNS = ((128, 1), (512, 4), (2048, 16))
ATT_BLOCK = 128
ATT_TILE = 2048
ATT_RES = 16
ROPE_THETA = 10000.0
EVEN_IN = 5 * HALF_WIDTH
ODD_IN = 5 * HALF_WIDTH
D_FF = 2816
DEEPNORM_ALPHA = (2 * DEPTH) ** 0.25
LN_EPS = 1e-5

VMEM_LIMIT = 56 * 1024 * 1024


def _params(*sem):
    return pltpu.CompilerParams(dimension_semantics=sem, vmem_limit_bytes=VMEM_LIMIT)


def _sigmoid(x):
    return 1.0 / (1.0 + jnp.exp(-x))


def _silu(x):
    return x * _sigmoid(x)


def _gelu_tanh(x):
    c = math.sqrt(2.0 / math.pi)
    return x * (0.5 * (1.0 + jnp.tanh(c * (x + 0.044715 * (x * x * x)))))


def _layer_norm(y, g, b):
    mu = jnp.mean(y, axis=-1, keepdims=True)
    d = y - mu
    var = jnp.mean(d * d, axis=-1, keepdims=True)
    return d * lax.rsqrt(var + LN_EPS) * g + b


def _rope(t, cos, sin_signed):
    return t * cos + pltpu.roll(t, HEAD_DIM // 2, axis=1) * sin_signed


def _dot(a, b):
    return jnp.dot(a, b, preferred_element_type=F32)


def _dot_nt(a, b):
    return lax.dot_general(a, b, (((1,), (1,)), ((), ())), preferred_element_type=F32)


def _rope_table_kernel(pos_ref, inv_ref, sign_ref, cos_ref, sin_ref):
    ang = pos_ref[...].astype(F32) * inv_ref[...]
    cos_ref[...] = jnp.cos(ang)
    sin_ref[...] = jnp.sin(ang) * sign_ref[...]


def _rope_tables(positions):
    B, S = positions.shape
    T = B * S
    ts = 2048
    half = HEAD_DIM // 2
    inv = ROPE_THETA ** (-jnp.arange(0, HEAD_DIM, 2, dtype=F32) / HEAD_DIM)
    inv2 = jnp.concatenate([inv, inv])[None, :]
    sign = jnp.concatenate([-jnp.ones((half,), F32), jnp.ones((half,), F32)])[None, :]
    row = pl.BlockSpec((1, HEAD_DIM), lambda i: (0, 0))
    cos, sin = pl.pallas_call(
        _rope_table_kernel,
        grid=(T // ts,),
        in_specs=[pl.BlockSpec((ts, 1), lambda i: (i, 0)), row, row],
        out_specs=[pl.BlockSpec((ts, HEAD_DIM), lambda i: (i, 0))] * 2,
        out_shape=[jax.ShapeDtypeStruct((T, HEAD_DIM), F32)] * 2,
        compiler_params=_params("arbitrary"),
        name="rope_tables",
    )(positions.reshape(T, 1), inv2, sign)
    return cos.reshape(B, S, HEAD_DIM), sin.reshape(B, S, HEAD_DIM)


def _seq_specs(B, nblk, ts):
    last = B * nblk - 1
    cur = lambda w: pl.BlockSpec((None, ts, w), lambda s: (s // nblk, s % nblk, 0))
    first = lambda w: pl.BlockSpec((None, ts, w), lambda s: (0, 0, 0))

    def ahead(w):
        def index_map(s):
            n = jnp.minimum(s + 1, last)
            return (n // nblk, n % nblk, 0)
        return pl.BlockSpec((None, ts, w), index_map)

    return cur, ahead, first


def _project_ahead(step, x0_ref, xn_ref, w_ref, xb_ref, za_ref, zb_ref, body):
    @pl.when(step == 0)
    def _():
        za_ref[...] = _dot(x0_ref[...].astype(BF16), w_ref[...])

    def run(z_cur_ref, z_next_ref):
        def project_piece(k, n):
            if k == 0:
                xb_ref[...] = xn_ref[...].astype(BF16)
            width = w_ref.shape[1] // n
            cols = slice(k * width, (k + 1) * width)
            z_next_ref[:, cols] = _dot(xb_ref[...], w_ref[:, cols])
        body(z_cur_ref, project_piece)

    @pl.when(step % 2 == 0)
    def _():
        run(za_ref, zb_ref)

    @pl.when(step % 2 == 1)
    def _():
        run(zb_ref, za_ref)


def _even_kernel(x0_ref, xn_ref, wi_ref, cos_ref, sin_ref, decay_ref, qdec_ref, kdec_ref, cdec_ref, gn_ref, pw_ref,
                 ps_ref, ret_ref, pool_ref, xb_ref, za_ref, zb_ref, state_ref, pbuf_ref, s2_ref, s4_ref, s8_ref,
                 *, ts, nblk):
    step = pl.program_id(0)
    sblk = step % nblk
    H = POOL_HALO

    @pl.when(sblk == 0)
    def _():
        state_ref[...] = jnp.zeros_like(state_ref)
        pbuf_ref[0:H, :] = jnp.zeros((H, HALF_WIDTH), F32)

    body = functools.partial(_even_body, cos_ref, sin_ref, decay_ref, qdec_ref, kdec_ref, cdec_ref, gn_ref, pw_ref,
                             ps_ref, ret_ref, pool_ref, state_ref, pbuf_ref, s2_ref, s4_ref, s8_ref, sblk, ts)
    _project_ahead(step, x0_ref, xn_ref, wi_ref, xb_ref, za_ref, zb_ref, body)


def _even_body(cos_ref, sin_ref, decay_ref, qdec_ref, kdec_ref, cdec_ref, gn_ref, pw_ref, ps_ref, ret_ref, pool_ref,
               state_ref, pbuf_ref, s2_ref, s4_ref, s8_ref, sblk, ts, z_ref, project_piece):
    H = POOL_HALO
    npieces = ts // RET_CHUNK + 1
    for c in range(ts // RET_CHUNK):
        project_piece(c, npieces)
        rows = pl.ds(c * RET_CHUNK, RET_CHUNK)
        cos = cos_ref[rows, :]
        sin = sin_ref[rows, :]
        for h in range(N_HEADS):
            lanes = slice(h * HEAD_DIM, (h + 1) * HEAD_DIM)
            q = _rope(z_ref[rows, lanes], cos, sin)
            k = _rope(z_ref[rows, pl.ds(HALF_WIDTH + h * HEAD_DIM, HEAD_DIM)], cos, sin) * (HEAD_DIM ** -0.5)
            v = z_ref[rows, pl.ds(2 * HALF_WIDTH + h * HEAD_DIM, HEAD_DIM)]
            gate = z_ref[rows, pl.ds(3 * HALF_WIDTH + h * HEAD_DIM, HEAD_DIM)]
            qb = q.astype(BF16)
            kb = k.astype(BF16)
            vb = v.astype(BF16)
            scores = _dot_nt(qb, kb) * decay_ref[h]
            inner = _dot(scores.astype(BF16), vb)
            state = state_ref[h]
            cross = _dot(qb, state.astype(BF16)) * qdec_ref[h]
            kd = (k * kdec_ref[h]).T.astype(BF16)
            state_ref[h] = state * cdec_ref[h] + _dot(kd, vb)
            y = inner + cross
            mu = jnp.mean(y, axis=-1, keepdims=True)
            d = y - mu
            var = jnp.mean(d * d, axis=-1, keepdims=True)
            yn = d * lax.rsqrt(var + LN_EPS) * gn_ref[:, lanes]
            ret_ref[rows, lanes] = (yn * _silu(gate)).astype(ret_ref.dtype)

    project_piece(npieces - 1, npieces)
    pbuf_ref[H:H + ts, :] = z_ref[:, pl.ds(4 * HALF_WIDTH, HALF_WIDTH)]
    G = POOL_GROUP
    s2_ref[8:H + ts, :] = pbuf_ref[8:H + ts, :] + pbuf_ref[7:H + ts - 1, :]
    s4_ref[16:H + ts, :] = s2_ref[16:H + ts, G:4 * G] + s2_ref[14:H + ts - 2, G:4 * G]
    s8_ref[24:H + ts, :] = s4_ref[24:H + ts, G:3 * G] + s4_ref[20:H + ts - 4, G:3 * G]
    s16 = s8_ref[32:H + ts, G:2 * G] + s8_ref[24:H + ts - 8, G:2 * G]
    wsums = (s2_ref[H:H + ts, 0:G], s4_ref[H:H + ts, 0:G], s8_ref[H:H + ts, 0:G], s16)
    tpos = sblk * ts + lax.broadcasted_iota(jnp.int32, (ts, 1), 0) + 1
    for gi, w in enumerate(POOL_WINDOWS):
        lanes = slice(gi * G, (gi + 1) * G)
        cnt = jnp.minimum(tpos, w).astype(F32)
        pooled = wsums[gi] / cnt - pbuf_ref[H:H + ts, lanes]
        mixed = _dot(pooled.astype(BF16), pw_ref[gi]) * ps_ref[:, lanes]
        pool_ref[:, lanes] = mixed.astype(pool_ref.dtype)
    pbuf_ref[0:H, :] = pbuf_ref[ts:ts + H, :]


def _even_core(x3, w_in_bf16, cos, sin, ret_norm_g, pool_w_bf16, pool_scale):
    B, S, D = x3.shape
    ts = 512
    C = RET_CHUNK
    lg = jnp.log1p(-(2.0 ** (-5.0 - jnp.arange(N_HEADS, dtype=F32))))
    idx = jnp.arange(C, dtype=F32)
    rel = idx[:, None] - idx[None, :]
    decay = jnp.where(rel[None] >= 0, jnp.exp(jnp.maximum(rel, 0.0)[None] * lg[:, None, None]), 0.0)
    k_decay = jnp.exp((C - 1 - idx)[None, :] * lg[:, None])
    q_decay = jnp.exp((idx + 1.0)[None, :] * lg[:, None])
    chunk_decay = jnp.exp(C * lg)
    qdec = jnp.broadcast_to(q_decay[:, :, None], (N_HEADS, C, HEAD_DIM))
    kdec = jnp.broadcast_to(k_decay[:, :, None], (N_HEADS, C, HEAD_DIM))
    cdec = jnp.broadcast_to(chunk_decay[:, None, None], (N_HEADS, 1, HEAD_DIM))

    const3 = lambda shape: pl.BlockSpec(shape, lambda s: (0, 0, 0), pipeline_mode=pl.Buffered(1))
    const2 = lambda shape: pl.BlockSpec(shape, lambda s: (0, 0), pipeline_mode=pl.Buffered(1))
    nblk = S // ts
    seq, ahead, first = _seq_specs(B, nblk, ts)
    return pl.pallas_call(
        functools.partial(_even_kernel, ts=ts, nblk=nblk),
        grid=(B * nblk,),
        in_specs=[first(D), ahead(D), const2((D, EVEN_IN)), seq(HEAD_DIM), seq(HEAD_DIM),
                  const3((N_HEADS, C, C)), const3((N_HEADS, C, HEAD_DIM)), const3((N_HEADS, C, HEAD_DIM)),
                  const3((N_HEADS, 1, HEAD_DIM)), const2((1, HALF_WIDTH)),
                  const3((len(POOL_WINDOWS), POOL_GROUP, POOL_GROUP)), const2((1, HALF_WIDTH))],
        out_specs=[seq(HALF_WIDTH), seq(HALF_WIDTH)],
        out_shape=[jax.ShapeDtypeStruct((B, S, HALF_WIDTH), BF16)] * 2,
        scratch_shapes=[pltpu.VMEM((ts, D), BF16), pltpu.VMEM((ts, EVEN_IN), F32), pltpu.VMEM((ts, EVEN_IN), F32),
                        pltpu.VMEM((N_HEADS, HEAD_DIM, HEAD_DIM), F32),
                        pltpu.VMEM((POOL_HALO + ts, HALF_WIDTH), F32),
                        pltpu.VMEM((POOL_HALO + ts, HALF_WIDTH), F32),
                        pltpu.VMEM((POOL_HALO + ts, 3 * POOL_GROUP), F32),
                        pltpu.VMEM((POOL_HALO + ts, 2 * POOL_GROUP), F32)],
        compiler_params=_params("arbitrary"),
        name="even_core",
    )(x3, x3, w_in_bf16, cos, sin, decay, qdec, kdec, cdec, ret_norm_g[None, :], pool_w_bf16, pool_scale[None, :])


def _lru_kernel(x0_ref, xn_ref, wi_ref, cw_ref, cb_ref, wa_ref, ba_ref, wx_ref, bx_ref, lam_ref, y_ref,
                xb_ref, za_ref, zb_ref, ubuf_ref, hcar_ref, h_ref, *, ts, nblk):
    step = pl.program_id(0)

    @pl.when(step % nblk == 0)
    def _():
        ubuf_ref[0:CONV_HALO, :] = jnp.zeros((CONV_HALO, HALF_WIDTH), F32)
        hcar_ref[...] = jnp.zeros_like(hcar_ref)

    body = functools.partial(_lru_body, cw_ref, cb_ref, wa_ref, ba_ref, wx_ref, bx_ref, lam_ref, y_ref,
                             ubuf_ref, hcar_ref, h_ref, ts)
    _project_ahead(step, x0_ref, xn_ref, wi_ref, xb_ref, za_ref, zb_ref, body)


def _lru_body(cw_ref, cb_ref, wa_ref, ba_ref, wx_ref, bx_ref, lam_ref, y_ref, ubuf_ref, hcar_ref, h_ref, ts,
              z_ref, project_piece):
    H = CONV_HALO
    npieces = 4
    project_piece(0, npieces)
    ubuf_ref[H:H + ts, :] = z_ref[:, pl.ds(HALF_WIDTH, HALF_WIDTH)]
    u = cb_ref[...] + cw_ref[CONV_WIDTH - 1:CONV_WIDTH, :] * ubuf_ref[H:H + ts, :]
    for k in range(CONV_WIDTH - 1):
        off = H - (CONV_WIDTH - 1) + k
        u = u + cw_ref[k:k + 1, :] * ubuf_ref[off:off + ts, :]
    ubuf_ref[0:H, :] = ubuf_ref[ts:ts + H, :]

    project_piece(1, npieces)
    ub = u.astype(BF16)
    r = _sigmoid(_dot(ub, wa_ref[...]) + ba_ref[...])
    ig = _sigmoid(_dot(ub, wx_ref[...]) + bx_ref[...])
    nl = -lam_ref[...]
    softplus = jnp.maximum(nl, 0.0) + jnp.log1p(jnp.exp(-jnp.abs(nl)))
    log_a = (-LRU_C * r) * softplus
    a = jnp.exp(log_a)
    bseq = jnp.sqrt(jnp.tanh(-log_a) * (1.0 + a * a)) * (ig * u)

    G = 8
    project_piece(2, npieces)
    a = a.reshape(ts // G, G, HALF_WIDTH)
    bseq = bseq.reshape(ts // G, G, HALF_WIDTH)
    row = lax.broadcasted_iota(jnp.int32, (1, G, 1), 1)
    sh = 1
    while sh < G:
        a_prev = pltpu.roll(a, sh, axis=1)
        b_prev = pltpu.roll(bseq, sh, axis=1)
        m = row >= sh
        bseq = jnp.where(m, a * b_prev + bseq, bseq)
        a = jnp.where(m, a * a_prev, a)
        sh *= 2
    project_piece(3, npieces)
    carry = hcar_ref[0:1, :]
    for v in range(ts // G):
        hv = bseq[v] + a[v] * carry
        h_ref[G * v:G * (v + 1), :] = hv
        carry = hv[G - 1:G]
    hcar_ref[...] = jnp.broadcast_to(carry, hcar_ref.shape)
    y_ref[...] = (h_ref[...] * _gelu_tanh(z_ref[:, 0:HALF_WIDTH])).astype(y_ref.dtype)


def _block_diag(w):
    n, c, d = w.shape
    eye = jnp.eye(n, dtype=w.dtype)
    return (w[:, :, None, :] * eye[:, None, :, None]).reshape(n * c, n * d)


def _lru_core(x3, w_in_bf16, conv_w, conv_b, gate_a_w, gate_a_b, gate_x_w, gate_x_b, lam):
    B, S, D = x3.shape
    ts = 512
    W = HALF_WIDTH
    wa = _block_diag(gate_a_w).astype(BF16)
    wx = _block_diag(gate_x_w).astype(BF16)
    const = lambda shape: pl.BlockSpec(shape, lambda s: (0, 0), pipeline_mode=pl.Buffered(1))
    nblk = S // ts
    seq, ahead, first = _seq_specs(B, nblk, ts)
    return pl.pallas_call(
        functools.partial(_lru_kernel, ts=ts, nblk=nblk),
        grid=(B * nblk,),
        in_specs=[first(D), ahead(D), const((D, 2 * W)),
                  const((CONV_WIDTH, W)), const((1, W)), const((W, W)), const((1, W)),
                  const((W, W)), const((1, W)), const((1, W))],
        out_specs=seq(W),
        out_shape=jax.ShapeDtypeStruct((B, S, W), BF16),
        scratch_shapes=[pltpu.VMEM((ts, D), BF16), pltpu.VMEM((ts, 2 * W), F32), pltpu.VMEM((ts, 2 * W), F32),
                        pltpu.VMEM((CONV_HALO + ts, W), F32), pltpu.VMEM((8, W), F32), pltpu.VMEM((ts, W), F32)],
        compiler_params=_params("arbitrary"),
        name="lru_core",
    )(x3, x3, w_in_bf16, conv_w, conv_b[None, :], wa, gate_a_b[None, :], wx, gate_x_b[None, :], lam[None, :])


def _qkv_rm_kernel(x_ref, w_ref, cos_ref, sin_ref, q_ref, k_ref, v_ref, xs_ref):
    R = ATT_RES
    rows_per = x_ref.shape[0] // R
    ncol = x_ref.shape[1] // HEAD_DIM

    def residue_rows(ref, r):
        return ref[pl.ds(r, rows_per, stride=R), :]

    for c in range(ncol):
        xs_ref[c] = x_ref[:, c * HEAD_DIM:(c + 1) * HEAD_DIM]
    qscale = HEAD_DIM ** -0.5 * math.log2(math.e)
    half = R // 2
    for r0 in (0, half):
        x = jnp.concatenate(
            [jnp.concatenate([residue_rows(xs_ref.at[c], r) for c in range(ncol)], axis=1)
             for r in range(r0, r0 + half)], axis=0).astype(BF16)
        z = _dot(x, w_ref[...])
        for r in range(r0, r0 + half):
            rows = slice((r - r0) * rows_per, (r - r0 + 1) * rows_per)
            cos = residue_rows(cos_ref, r)
            sin = residue_rows(sin_ref, r)
            for h in range(N_HEADS):
                lanes = slice(h * HEAD_DIM, (h + 1) * HEAD_DIM)
                q_ref[r, :, lanes] = _rope(z[rows, h * HEAD_DIM:(h + 1) * HEAD_DIM], cos, sin) * qscale
                k_ref[r, :, lanes] = _rope(z[rows, HALF_WIDTH + h * HEAD_DIM:HALF_WIDTH + (h + 1) * HEAD_DIM],
                                           cos, sin)
            v_ref[r] = z[rows, 2 * HALF_WIDTH:3 * HALF_WIDTH]


def _qkv_rm(x3, w_bf16, cos, sin):
    B, S, D = x3.shape
    R = ATT_RES
    I = S // R
    W = HALF_WIDTH
    ts = 1024
    seq = lambda w: pl.BlockSpec((None, ts, w), lambda b, t: (b, t, 0))
    out_spec = pl.BlockSpec((None, R, ts // R, W), lambda b, t: (b, 0, t, 0))
    return pl.pallas_call(
        _qkv_rm_kernel,
        grid=(B, S // ts),
        in_specs=[seq(D), pl.BlockSpec((D, 3 * W), lambda b, t: (0, 0), pipeline_mode=pl.Buffered(1)),
                  seq(HEAD_DIM), seq(HEAD_DIM)],
        out_specs=[out_spec] * 3,
        out_shape=[jax.ShapeDtypeStruct((B, R, I, W), F32)] * 3,
        scratch_shapes=[pltpu.VMEM((D // HEAD_DIM, ts, HEAD_DIM), F32)],
        compiler_params=_params("arbitrary", "arbitrary"),
        name="qkv_rm",
    )(x3, w_bf16, cos, sin)


def _pattern_bias(dil):
    nslab = ATT_RES // dil
    slab = ATT_BLOCK // nslab
    row = np.arange(ATT_BLOCK)
    col = np.arange(2 * ATT_BLOCK)
    jq = nslab * (row % slab) + row // slab + ATT_BLOCK
    jk = nslab * (col % slab) + (col % ATT_BLOCK) // slab + ATT_BLOCK * (col // ATT_BLOCK)
    delta = jq[:, None] - jk[None, :]
    return np.where((delta >= 0) & (delta <= ATT_BLOCK), 0.0, -np.inf).astype(np.float32)


def _att_rm_kernel(q_ref, k_ref, v_ref, bias_ref, o_ref, num_ref, den_ref, mx_ref, onat_ref):
    QB = ATT_BLOCK
    TI = ATT_TILE // ATT_RES
    ones = jnp.ones((2 * QB, HEAD_DIM), BF16)

    def gather(ref, slabs, r0, rn):
        return jnp.concatenate([ref[r, r0:r0 + rn, :] for r in slabs], axis=0)

    for tile in range(q_ref.shape[1] // TI):
        i0 = tile * TI
        for p, (window, dil) in enumerate(DIL_PATTERNS):
            assert window // dil == QB
            nslab = ATT_RES // dil
            slab = QB // nslab
            for n in range(ATT_TILE // QB):
                res, m = n % dil, n // dil
                slabs = [res + dil * a for a in range(nslab)]
                c0 = i0 + slab * m
                qb = gather(q_ref, slabs, c0, slab).astype(BF16)
                if c0 == 0:
                    kb = gather(k_ref, slabs, c0, slab).astype(BF16)
                    vb = gather(v_ref, slabs, c0, slab).astype(BF16)
                    bias = bias_ref[p, :, QB:2 * QB]
                else:
                    kb = jnp.concatenate([gather(k_ref, slabs, c0 - slab, slab),
                                          gather(k_ref, slabs, c0, slab)], axis=0).astype(BF16)
                    vb = jnp.concatenate([gather(v_ref, slabs, c0 - slab, slab),
                                          gather(v_ref, slabs, c0, slab)], axis=0).astype(BF16)
                    bias = bias_ref[p]
                s = _dot_nt(qb, kb) + bias
                mx = jnp.max(s, axis=-1, keepdims=True)
                e = jnp.exp2(s - mx).astype(BF16)
                nd = _dot(e, jnp.concatenate([vb, ones[0:vb.shape[0]]], axis=1))
                mxb = jnp.broadcast_to(mx, (QB, HEAD_DIM))
                for a, r in enumerate(slabs):
                    rows = slice(a * slab, (a + 1) * slab)
                    dst = slice(slab * m, slab * (m + 1))
                    num_ref[p, r, dst, :] = nd[rows, 0:HEAD_DIM]
                    den_ref[p, r, dst, :] = nd[rows, HEAD_DIM:2 * HEAD_DIM]
                    mx_ref[p, r, dst, :] = mxb[rows]

        m0, m1, m2 = mx_ref[0], mx_ref[1], mx_ref[2]
        mm = jnp.maximum(jnp.maximum(m0, m1), m2)
        w0, w1, w2 = jnp.exp2(m0 - mm), jnp.exp2(m1 - mm), jnp.exp2(m2 - mm)
        num = w0 * num_ref[0] + w1 * num_ref[1] + w2 * num_ref[2]
        den = w0 * den_ref[0] + w1 * den_ref[1] + w2 * den_ref[2]
        out = num / den
        for r in range(ATT_RES):
            onat_ref[pl.ds(r, TI, stride=ATT_RES), :] = out[r]
        o_ref[tile * ATT_TILE:(tile + 1) * ATT_TILE, :] = onat_ref[...].astype(o_ref.dtype)


def _att_rm(q, k, v):
    B, R, I, W = q.shape
    TI = ATT_TILE // ATT_RES
    assert R == ATT_RES and I % TI == 0
    bias = jnp.asarray(np.stack([_pattern_bias(dil) for _, dil in DIL_PATTERNS]))
    blk = pl.BlockSpec((None, R, I, HEAD_DIM), lambda b, h: (b, 0, 0, h))
    acc = pltpu.VMEM((len(DIL_PATTERNS), R, TI, HEAD_DIM), F32)
    return pl.pallas_call(
        _att_rm_kernel,
        grid=(B, N_HEADS),
        in_specs=[blk, blk, blk,
                  pl.BlockSpec(bias.shape, lambda b, h: (0, 0, 0), pipeline_mode=pl.Buffered(1))],
        out_specs=pl.BlockSpec((None, R * I, HEAD_DIM), lambda b, h: (b, 0, h)),
        out_shape=jax.ShapeDtypeStruct((B, R * I, W), BF16),
        scratch_shapes=[acc, acc, acc, pltpu.VMEM((ATT_TILE, HEAD_DIM), F32)],
        compiler_params=_params("arbitrary", "arbitrary"),
        name="att_rm",
    )(q, k, v, bias)


FFN_SUB = 512
FFN_CHUNK = 256


def _mix_ffn_kernel(x_ref, a1_ref, a2_ref, wm_ref, wi_ref, wo_ref, g_ref, b_ref, o_ref, *, tm):
    W = HALF_WIDTH
    for r0 in range(0, tm, FFN_SUB):
        rows = slice(r0, r0 + FFN_SUB)
        mix = _dot(a1_ref[rows, :], wm_ref[0:W, :]) + _dot(a2_ref[rows, :], wm_ref[W:2 * W, :])
        h1 = _layer_norm(DEEPNORM_ALPHA * x_ref[rows, :] + mix, g_ref[0:1, :], b_ref[0:1, :])
        hb = h1.astype(BF16)
        acc = None
        for c0 in range(0, D_FF, FFN_CHUNK):
            gate = _dot(hb, wi_ref[:, c0:c0 + FFN_CHUNK])
            up = _dot(hb, wi_ref[:, D_FF + c0:D_FF + c0 + FFN_CHUNK])
            part = _dot((_silu(gate) * up).astype(BF16), wo_ref[c0:c0 + FFN_CHUNK, :])
            acc = part if acc is None else acc + part
        o_ref[rows, :] = _layer_norm(DEEPNORM_ALPHA * h1 + acc, g_ref[1:2, :], b_ref[1:2, :])


def _mix_ffn(x2d, a1, a2, w_mix_bf16, w_in_bf16, w_out_bf16, g2, b2):
    T, D = x2d.shape
    tm = 1024
    rowblk = lambda w: pl.BlockSpec((tm, w), lambda i: (i, 0))
    resident = lambda shape: pl.BlockSpec(shape, lambda i: (0, 0), pipeline_mode=pl.Buffered(1))
    return pl.pallas_call(
        functools.partial(_mix_ffn_kernel, tm=tm),
        grid=(T // tm,),
        in_specs=[rowblk(D), rowblk(HALF_WIDTH), rowblk(HALF_WIDTH), resident((D, D)),
                  resident((D, 2 * D_FF)), resident((D_FF, D)), resident((2, D)), resident((2, D))],
        out_specs=rowblk(D),
        out_shape=jax.ShapeDtypeStruct((T, D), F32),
        compiler_params=_params("arbitrary"),
        name="mix_ffn",
    )(x2d, a1, a2, w_mix_bf16, w_in_bf16, w_out_bf16, g2, b2)


def kernel(x, positions, ev_w_in, ev_ret_norm_g, ev_pool_w, ev_pool_scale, ev_w_out, od_w_in, od_conv_w, od_conv_b, od_gate_a_w, od_gate_a_b, od_gate_x_w, od_gate_x_b, od_lru_lambda, od_w_out, ffn_w_in, ffn_w_out, ln_g, ln_b):
    B, S, D = x.shape
    T = B * S
    cos, sin = _rope_tables(positions)
    h = x.reshape(T, D)
    for layer in range(DEPTH):
        j = layer // 2
        h3 = h.reshape(B, S, D)
        if layer % 2 == 0:
            a1, a2 = _even_core(h3, ev_w_in[j].astype(BF16), cos, sin, ev_ret_norm_g[j],
                                ev_pool_w[j].astype(BF16), ev_pool_scale[j])
            w_mix = ev_w_out[j]
        else:
            w_in = od_w_in[j].astype(BF16)
            a1 = _lru_core(h3, w_in[:, :2 * HALF_WIDTH], od_conv_w[j], od_conv_b[j], od_gate_a_w[j], od_gate_a_b[j],
                           od_gate_x_w[j], od_gate_x_b[j], od_lru_lambda[j])
            a2 = _att_rm(*_qkv_rm(h3, w_in[:, 2 * HALF_WIDTH:], cos, sin))
            w_mix = od_w_out[j]
        h = _mix_ffn(h, a1.reshape(T, HALF_WIDTH), a2.reshape(T, HALF_WIDTH), w_mix.astype(BF16),
                     ffn_w_in[layer].astype(BF16), ffn_w_out[layer].astype(BF16), ln_g[layer], ln_b[layer])
    return h.reshape(B, S, D)
```

```python
import functools
import math

import jax
import jax.numpy as jnp
import numpy as np
from jax import lax
from jax.experimental import pallas as pl
from jax.experimental.pallas import tpu as pltpu

F32 = jnp.float32
BF16 = jnp.bfloat16

D_MODEL = 1024
DEPTH = 4
HEAD_DIM = 128
HALF_WIDTH = 512
N_HEADS = HALF_WIDTH // HEAD_DIM
RET_CHUNK = 128
POOL_WINDOWS = (2, 4, 8, 16)
POOL_GROUP = 128
POOL_HALO = 32
LRU_BLOCKS = 8
LRU_C = 8.0
CONV_WIDTH = 4
CONV_HALO = 8
DIL_PATTERNS = ((128, 1), (512, 4), (2048, 16))
ATT_BLOCK = 128
ATT_TILE = 2048
ATT_RES = 16
ROPE_THETA = 10000.0
EVEN_IN = 5 * HALF_WIDTH
ODD_IN = 5 * HALF_WIDTH
D_FF = 2816
DEEPNORM_ALPHA = (2 * DEPTH) ** 0.25
LN_EPS = 1e-5

VMEM_LIMIT = 56 * 1024 * 1024


def _params(*sem):
    return pltpu.CompilerParams(dimension_semantics=sem, vmem_limit_bytes=VMEM_LIMIT)


def _sigmoid(x):
    return 1.0 / (1.0 + jnp.exp(-x))


def _silu(x):
    return x * _sigmoid(x)


def _gelu_tanh(x):
    c = math.sqrt(2.0 / math.pi)
    return x * (0.5 * (1.0 + jnp.tanh(c * (x + 0.044715 * (x * x * x)))))


def _layer_norm(y, g, b):
    mu = jnp.mean(y, axis=-1, keepdims=True)
    d = y - mu
    var = jnp.mean(d * d, axis=-1, keepdims=True)
    return d * lax.rsqrt(var + LN_EPS) * g + b


def _rope(t, cos, sin_signed):
    return t * cos + pltpu.roll(t, HEAD_DIM // 2, axis=1) * sin_signed


def _dot(a, b):
    return jnp.dot(a, b, preferred_element_type=F32)


def _dot_nt(a, b):
    return lax.dot_general(a, b, (((1,), (1,)), ((), ())), preferred_element_type=F32)


def _rope_table_kernel(pos_ref, inv_ref, sign_ref, cos_ref, sin_ref):
    ang = pos_ref[...].astype(F32) * inv_ref[...]
    cos_ref[...] = jnp.cos(ang)
    sin_ref[...] = jnp.sin(ang) * sign_ref[...]


def _rope_tables(positions):
    B, S = positions.shape
    T = B * S
    ts = 2048
    half = HEAD_DIM // 2
    inv = ROPE_THETA ** (-jnp.arange(0, HEAD_DIM, 2, dtype=F32) / HEAD_DIM)
    inv2 = jnp.concatenate([inv, inv])[None, :]
    sign = jnp.concatenate([-jnp.ones((half,), F32), jnp.ones((half,), F32)])[None, :]
    row = pl.BlockSpec((1, HEAD_DIM), lambda i: (0, 0))
    cos, sin = pl.pallas_call(
        _rope_table_kernel,
        grid=(T // ts,),
        in_specs=[pl.BlockSpec((ts, 1), lambda i: (i, 0)), row, row],
        out_specs=[pl.BlockSpec((ts, HEAD_DIM), lambda i: (i, 0))] * 2,
        out_shape=[jax.ShapeDtypeStruct((T, HEAD_DIM), F32)] * 2,
        compiler_params=_params("arbitrary"),
        name="rope_tables",
    )(positions.reshape(T, 1), inv2, sign)
    return cos.reshape(B, S, HEAD_DIM), sin.reshape(B, S, HEAD_DIM)


def _seq_specs(B, nblk, ts):
    last = B * nblk - 1
    cur = lambda w: pl.BlockSpec((None, ts, w), lambda s: (s // nblk, s % nblk, 0))
    first = lambda w: pl.BlockSpec((None, ts, w), lambda s: (0, 0, 0))

    def ahead(w):
        def index_map(s):
            n = jnp.minimum(s + 1, last)
            return (n // nblk, n % nblk, 0)
        return pl.BlockSpec((None, ts, w), index_map)

    return cur, ahead, first


def _project_ahead(step, x0_ref, xn_ref, w_ref, xb_ref, za_ref, zb_ref, body):
    @pl.when(step == 0)
    def _():
        za_ref[...] = _dot(x0_ref[...].astype(BF16), w_ref[...])

    def run(z_cur_ref, z_next_ref):
        def project_piece(k, n):
            if k == 0:
                xb_ref[...] = xn_ref[...].astype(BF16)
            width = w_ref.shape[1] // n
            cols = slice(k * width, (k + 1) * width)
            z_next_ref[:, cols] = _dot(xb_ref[...], w_ref[:, cols])
        body(z_cur_ref, project_piece)

    @pl.when(step % 2 == 0)
    def _():
        run(za_ref, zb_ref)

    @pl.when(step % 2 == 1)
    def _():
        run(zb_ref, za_ref)


def _even_kernel(x0_ref, xn_ref, wi_ref, cos_ref, sin_ref, decay_ref, qdec_ref, kdec_ref, cdec_ref, gn_ref, pw_ref,
                 ps_ref, ret_ref, pool_ref, xb_ref, za_ref, zb_ref, state_ref, pbuf_ref, s2_ref, s4_ref, s8_ref,
                 *, ts, nblk):
    step = pl.program_id(0)
    sblk = step % nblk
    H = POOL_HALO

    @pl.when(sblk == 0)
    def _():
        state_ref[...] = jnp.zeros_like(state_ref)
        pbuf_ref[0:H, :] = jnp.zeros((H, HALF_WIDTH), F32)

    body = functools.partial(_even_body, cos_ref, sin_ref, decay_ref, qdec_ref, kdec_ref, cdec_ref, gn_ref, pw_ref,
                             ps_ref, ret_ref, pool_ref, state_ref, pbuf_ref, s2_ref, s4_ref, s8_ref, sblk, ts)
    _project_ahead(step, x0_ref, xn_ref, wi_ref, xb_ref, za_ref, zb_ref, body)


def _even_body(cos_ref, sin_ref, decay_ref, qdec_ref, kdec_ref, cdec_ref, gn_ref, pw_ref, ps_ref, ret_ref, pool_ref,
               state_ref, pbuf_ref, s2_ref, s4_ref, s8_ref, sblk, ts, z_ref, project_piece):
    H = POOL_HALO
    npieces = ts // RET_CHUNK + 1
    for c in range(ts // RET_CHUNK):
        project_piece(c, npieces)
        rows = pl.ds(c * RET_CHUNK, RET_CHUNK)
        cos = cos_ref[rows, :]
        sin = sin_ref[rows, :]
        for h in range(N_HEADS):
            lanes = slice(h * HEAD_DIM, (h + 1) * HEAD_DIM)
            q = _rope(z_ref[rows, lanes], cos, sin)
            k = _rope(z_ref[rows, pl.ds(HALF_WIDTH + h * HEAD_DIM, HEAD_DIM)], cos, sin) * (HEAD_DIM ** -0.5)
            v = z_ref[rows, pl.ds(2 * HALF_WIDTH + h * HEAD_DIM, HEAD_DIM)]
            gate = z_ref[rows, pl.ds(3 * HALF_WIDTH + h * HEAD_DIM, HEAD_DIM)]
            qb = q.astype(BF16)
            kb = k.astype(BF16)
            vb = v.astype(BF16)
            scores = _dot_nt(qb, kb) * decay_ref[h]
            inner = _dot(scores.astype(BF16), vb)
            state = state_ref[h]
            cross = _dot(qb, state.astype(BF16)) * qdec_ref[h]
            kd = (k * kdec_ref[h]).T.astype(BF16)
            state_ref[h] = state * cdec_ref[h] + _dot(kd, vb)
            y = inner + cross
            mu = jnp.mean(y, axis=-1, keepdims=True)
            d = y - mu
            var = jnp.mean(d * d, axis=-1, keepdims=True)
            yn = d * lax.rsqrt(var + LN_EPS) * gn_ref[:, lanes]
            ret_ref[rows, lanes] = (yn * _silu(gate)).astype(ret_ref.dtype)

    project_piece(npieces - 1, npieces)
    pbuf_ref[H:H + ts, :] = z_ref[:, pl.ds(4 * HALF_WIDTH, HALF_WIDTH)]
    G = POOL_GROUP
    s2_ref[8:H + ts, :] = pbuf_ref[8:H + ts, :] + pbuf_ref[7:H + ts - 1, :]
    s4_ref[16:H + ts, :] = s2_ref[16:H + ts, G:4 * G] + s2_ref[14:H + ts - 2, G:4 * G]
    s8_ref[24:H + ts, :] = s4_ref[24:H + ts, G:3 * G] + s4_ref[20:H + ts - 4, G:3 * G]
    s16 = s8_ref[32:H + ts, G:2 * G] + s8_ref[24:H + ts - 8, G:2 * G]
    wsums = (s2_ref[H:H + ts, 0:G], s4_ref[H:H + ts, 0:G], s8_ref[H:H + ts, 0:G], s16)
    tpos = sblk * ts + lax.broadcasted_iota(jnp.int32, (ts, 1), 0) + 1
    for gi, w in enumerate(POOL_WINDOWS):
        lanes = slice(gi * G, (gi + 1) * G)
        cnt = jnp.minimum(tpos, w).astype(F32)
        pooled = wsums[gi] / cnt - pbuf_ref[H:H + ts, lanes]
        mixed = _dot(pooled.astype(BF16), pw_ref[gi]) * ps_ref[:, lanes]
        pool_ref[:, lanes] = mixed.astype(pool_ref.dtype)
    pbuf_ref[0:H, :] = pbuf_ref[ts:ts + H, :]


def _even_core(x3, w_in_bf16, cos, sin, ret_norm_g, pool_w_bf16, pool_scale):
    B, S, D = x3.shape
    ts = 512
    C = RET_CHUNK
    lg = jnp.log1p(-(2.0 ** (-5.0 - jnp.arange(N_HEADS, dtype=F32))))
    idx = jnp.arange(C, dtype=F32)
    rel = idx[:, None] - idx[None, :]
    decay = jnp.where(rel[None] >= 0, jnp.exp(jnp.maximum(rel, 0.0)[None] * lg[:, None, None]), 0.0)
    k_decay = jnp.exp((C - 1 - idx)[None, :] * lg[:, None])
    q_decay = jnp.exp((idx + 1.0)[None, :] * lg[:, None])
    chunk_decay = jnp.exp(C * lg)
    qdec = jnp.broadcast_to(q_decay[:, :, None], (N_HEADS, C, HEAD_DIM))
    kdec = jnp.broadcast_to(k_decay[:, :, None], (N_HEADS, C, HEAD_DIM))
    cdec = jnp.broadcast_to(chunk_decay[:, None, None], (N_HEADS, 1, HEAD_DIM))

    const3 = lambda shape: pl.BlockSpec(shape, lambda s: (0, 0, 0), pipeline_mode=pl.Buffered(1))
    const2 = lambda shape: pl.BlockSpec(shape, lambda s: (0, 0), pipeline_mode=pl.Buffered(1))
    nblk = S // ts
    seq, ahead, first = _seq_specs(B, nblk, ts)
    return pl.pallas_call(
        functools.partial(_even_kernel, ts=ts, nblk=nblk),
        grid=(B * nblk,),
        in_specs=[first(D), ahead(D), const2((D, EVEN_IN)), seq(HEAD_DIM), seq(HEAD_DIM),
                  const3((N_HEADS, C, C)), const3((N_HEADS, C, HEAD_DIM)), const3((N_HEADS, C, HEAD_DIM)),
                  const3((N_HEADS, 1, HEAD_DIM)), const2((1, HALF_WIDTH)),
                  const3((len(POOL_WINDOWS), POOL_GROUP, POOL_GROUP)), const2((1, HALF_WIDTH))],
        out_specs=[seq(HALF_WIDTH), seq(HALF_WIDTH)],
        out_shape=[jax.ShapeDtypeStruct((B, S, HALF_WIDTH), BF16)] * 2,
        scratch_shapes=[pltpu.VMEM((ts, D), BF16), pltpu.VMEM((ts, EVEN_IN), F32), pltpu.VMEM((ts, EVEN_IN), F32),
                        pltpu.VMEM((N_HEADS, HEAD_DIM, HEAD_DIM), F32),
                        pltpu.VMEM((POOL_HALO + ts, HALF_WIDTH), F32),
                        pltpu.VMEM((POOL_HALO + ts, HALF_WIDTH), F32),
                        pltpu.VMEM((POOL_HALO + ts, 3 * POOL_GROUP), F32),
                        pltpu.VMEM((POOL_HALO + ts, 2 * POOL_GROUP), F32)],
        compiler_params=_params("arbitrary"),
        name="even_core",
    )(x3, x3, w_in_bf16, cos, sin, decay, qdec, kdec, cdec, ret_norm_g[None, :], pool_w_bf16, pool_scale[None, :])


def _lru_kernel(x_ref, wi_ref, cw_ref, cb_ref, wa_ref, ba_ref, wx_ref, bx_ref, lam_ref, y_ref,
                z_ref, ubuf_ref, hcar_ref, h_ref, *, ts, nblk):
    step = pl.program_id(0)
    H = CONV_HALO

    @pl.when(step % nblk == 0)
    def _():
        ubuf_ref[0:H, :] = jnp.zeros((H, HALF_WIDTH), F32)
        hcar_ref[...] = jnp.zeros_like(hcar_ref)

    z_ref[...] = _dot(x_ref[...].astype(BF16), wi_ref[...])
    ubuf_ref[H:H + ts, :] = z_ref[:, pl.ds(HALF_WIDTH, HALF_WIDTH)]
    u = cb_ref[...] + cw_ref[CONV_WIDTH - 1:CONV_WIDTH, :] * ubuf_ref[H:H + ts, :]
    for k in range(CONV_WIDTH - 1):
        off = H - (CONV_WIDTH - 1) + k
        u = u + cw_ref[k:k + 1, :] * ubuf_ref[off:off + ts, :]
    ubuf_ref[0:H, :] = ubuf_ref[ts:ts + H, :]

    ub = u.astype(BF16)
    r = _sigmoid(_dot(ub, wa_ref[...]) + ba_ref[...])
    ig = _sigmoid(_dot(ub, wx_ref[...]) + bx_ref[...])
    nl = -lam_ref[...]
    softplus = jnp.maximum(nl, 0.0) + jnp.log1p(jnp.exp(-jnp.abs(nl)))
    log_a = (-LRU_C * r) * softplus
    a = jnp.exp(log_a)
    bseq = jnp.sqrt(jnp.tanh(-log_a) * (1.0 + a * a)) * (ig * u)

    G = 8
    a = a.reshape(ts // G, G, HALF_WIDTH)
    bseq = bseq.reshape(ts // G, G, HALF_WIDTH)
    row = lax.broadcasted_iota(jnp.int32, (1, G, 1), 1)
    sh = 1
    while sh < G:
        a_prev = pltpu.roll(a, sh, axis=1)
        b_prev = pltpu.roll(bseq, sh, axis=1)
        m = row >= sh
        bseq = jnp.where(m, a * b_prev + bseq, bseq)
        a = jnp.where(m, a * a_prev, a)
        sh *= 2
    carry = hcar_ref[0:1, :]
    for v in range(ts // G):
        hv = bseq[v] + a[v] * carry
        h_ref[G * v:G * (v + 1), :] = hv
        carry = hv[G - 1:G]
    hcar_ref[...] = jnp.broadcast_to(carry, hcar_ref.shape)
    y_ref[...] = (h_ref[...] * _gelu_tanh(z_ref[:, 0:HALF_WIDTH])).astype(y_ref.dtype)


def _block_diag(w):
    n, c, d = w.shape
    eye = jnp.eye(n, dtype=w.dtype)
    return (w[:, :, None, :] * eye[:, None, :, None]).reshape(n * c, n * d)


def _lru_core(x3, w_in_bf16, conv_w, conv_b, gate_a_w, gate_a_b, gate_x_w, gate_x_b, lam):
    B, S, D = x3.shape
    ts = 512
    W = HALF_WIDTH
    wa = _block_diag(gate_a_w).astype(BF16)
    wx = _block_diag(gate_x_w).astype(BF16)
    const = lambda shape: pl.BlockSpec(shape, lambda s: (0, 0), pipeline_mode=pl.Buffered(1))
    nblk = S // ts
    seq, _, _ = _seq_specs(B, nblk, ts)
    return pl.pallas_call(
        functools.partial(_lru_kernel, ts=ts, nblk=nblk),
        grid=(B * nblk,),
        in_specs=[seq(D), const((D, 2 * W)),
                  const((CONV_WIDTH, W)), const((1, W)), const((W, W)), const((1, W)),
                  const((W, W)), const((1, W)), const((1, W))],
        out_specs=seq(W),
        out_shape=jax.ShapeDtypeStruct((B, S, W), BF16),
        scratch_shapes=[pltpu.VMEM((ts, 2 * W), F32),
                        pltpu.VMEM((CONV_HALO + ts, W), F32), pltpu.VMEM((8, W), F32), pltpu.VMEM((ts, W), F32)],
        compiler_params=_params("arbitrary"),
        name="lru_core",
    )(x3, w_in_bf16, conv_w, conv_b[None, :], wa, gate_a_b[None, :], wx, gate_x_b[None, :], lam[None, :])


def _qkv_rm_kernel(x_ref, w_ref, cos_ref, sin_ref, q_ref, k_ref, v_ref, xs_ref):
    R = ATT_RES
    rows_per = x_ref.shape[0] // R
    ncol = x_ref.shape[1] // HEAD_DIM

    def residue_rows(ref, r):
        return ref[pl.ds(r, rows_per, stride=R), :]

    for c in range(ncol):
        xs_ref[c] = x_ref[:, c * HEAD_DIM:(c + 1) * HEAD_DIM]
    qscale = HEAD_DIM ** -0.5 * math.log2(math.e)
    half = R // 2
    for r0 in (0, half):
        x = jnp.concatenate(
            [jnp.concatenate([residue_rows(xs_ref.at[c], r) for c in range(ncol)], axis=1)
             for r in range(r0, r0 + half)], axis=0).astype(BF16)
        z = _dot(x, w_ref[...])
        for r in range(r0, r0 + half):
            rows = slice((r - r0) * rows_per, (r - r0 + 1) * rows_per)
            cos = residue_rows(cos_ref, r)
            sin = residue_rows(sin_ref, r)
            for h in range(N_HEADS):
                lanes = slice(h * HEAD_DIM, (h + 1) * HEAD_DIM)
                q_ref[r, :, lanes] = _rope(z[rows, h * HEAD_DIM:(h + 1) * HEAD_DIM], cos, sin) * qscale
                k_ref[r, :, lanes] = _rope(z[rows, HALF_WIDTH + h * HEAD_DIM:HALF_WIDTH + (h + 1) * HEAD_DIM],
                                           cos, sin)
            v_ref[r] = z[rows, 2 * HALF_WIDTH:3 * HALF_WIDTH]


def _qkv_rm(x3, w_bf16, cos, sin):
    B, S, D = x3.shape
    R = ATT_RES
    I = S // R
    W = HALF_WIDTH
    ts = 1024
    seq = lambda w: pl.BlockSpec((None, ts, w), lambda b, t: (b, t, 0))
    out_spec = pl.BlockSpec((None, R, ts // R, W), lambda b, t: (b, 0, t, 0))
    return pl.pallas_call(
        _qkv_rm_kernel,
        grid=(B, S // ts),
        in_specs=[seq(D), pl.BlockSpec((D, 3 * W), lambda b, t: (0, 0), pipeline_mode=pl.Buffered(1)),
                  seq(HEAD_DIM), seq(HEAD_DIM)],
        out_specs=[out_spec] * 3,
        out_shape=[jax.ShapeDtypeStruct((B, R, I, W), F32)] * 3,
        scratch_shapes=[pltpu.VMEM((D // HEAD_DIM, ts, HEAD_DIM), F32)],
        compiler_params=_params("arbitrary", "arbitrary"),
        name="qkv_rm",
    )(x3, w_bf16, cos, sin)


def _pattern_bias(dil):
    nslab = ATT_RES // dil
    slab = ATT_BLOCK // nslab
    row = np.arange(ATT_BLOCK)
    col = np.arange(2 * ATT_BLOCK)
    jq = nslab * (row % slab) + row // slab + ATT_BLOCK
    jk = nslab * (col % slab) + (col % ATT_BLOCK) // slab + ATT_BLOCK * (col // ATT_BLOCK)
    delta = jq[:, None] - jk[None, :]
    return np.where((delta >= 0) & (delta <= ATT_BLOCK), 0.0, -np.inf).astype(np.float32)


def _att_rm_kernel(q_ref, k_ref, v_ref, bias_ref, o_ref, num_ref, den_ref, mx_ref, onat_ref):
    QB = ATT_BLOCK
    TI = ATT_TILE // ATT_RES
    ones = jnp.ones((2 * QB, HEAD_DIM), BF16)

    def gather(ref, slabs, r0, rn):
        return jnp.concatenate([ref[r, r0:r0 + rn, :] for r in slabs], axis=0)

    for tile in range(q_ref.shape[1] // TI):
        i0 = tile * TI
        for p, (window, dil) in enumerate(DIL_PATTERNS):
            assert window // dil == QB
            nslab = ATT_RES // dil
            slab = QB // nslab
            for n in range(ATT_TILE // QB):
                res, m = n % dil, n // dil
                slabs = [res + dil * a for a in range(nslab)]
                c0 = i0 + slab * m
                qb = gather(q_ref, slabs, c0, slab).astype(BF16)
                if c0 == 0:
                    kb = gather(k_ref, slabs, c0, slab).astype(BF16)
                    vb = gather(v_ref, slabs, c0, slab).astype(BF16)
                    bias = bias_ref[p, :, QB:2 * QB]
                else:
                    kb = jnp.concatenate([gather(k_ref, slabs, c0 - slab, slab),
                                          gather(k_ref, slabs, c0, slab)], axis=0).astype(BF16)
                    vb = jnp.concatenate([gather(v_ref, slabs, c0 - slab, slab),
                                          gather(v_ref, slabs, c0, slab)], axis=0).astype(BF16)
                    bias = bias_ref[p]
                s = _dot_nt(qb, kb) + bias
                mx = jnp.max(s, axis=-1, keepdims=True)
                e = jnp.exp2(s - mx).astype(BF16)
                nd = _dot(e, jnp.concatenate([vb, ones[0:vb.shape[0]]], axis=1))
                mxb = jnp.broadcast_to(mx, (QB, HEAD_DIM))
                for a, r in enumerate(slabs):
                    rows = slice(a * slab, (a + 1) * slab)
                    dst = slice(slab * m, slab * (m + 1))
                    num_ref[p, r, dst, :] = nd[rows, 0:HEAD_DIM]
                    den_ref[p, r, dst, :] = nd[rows, HEAD_DIM:2 * HEAD_DIM]
                    mx_ref[p, r, dst, :] = mxb[rows]

        m0, m1, m2 = mx_ref[0], mx_ref[1], mx_ref[2]
        mm = jnp.maximum(jnp.maximum(m0, m1), m2)
        w0, w1, w2 = jnp.exp2(m0 - mm), jnp.exp2(m1 - mm), jnp.exp2(m2 - mm)
        num = w0 * num_ref[0] + w1 * num_ref[1] + w2 * num_ref[2]
        den = w0 * den_ref[0] + w1 * den_ref[1] + w2 * den_ref[2]
        out = num / den
        for r in range(ATT_RES):
            onat_ref[pl.ds(r, TI, stride=ATT_RES), :] = out[r]
        o_ref[tile * ATT_TILE:(tile + 1) * ATT_TILE, :] = onat_ref[...].astype(o_ref.dtype)


def _att_rm(q, k, v):
    B, R, I, W = q.shape
    TI = ATT_TILE // ATT_RES
    assert R == ATT_RES and I % TI == 0
    bias = jnp.asarray(np.stack([_pattern_bias(dil) for _, dil in DIL_PATTERNS]))
    blk = pl.BlockSpec((None, R, I, HEAD_DIM), lambda b, h: (b, 0, 0, h))
    acc = pltpu.VMEM((len(DIL_PATTERNS), R, TI, HEAD_DIM), F32)
    return pl.pallas_call(
        _att_rm_kernel,
        grid=(B, N_HEADS),
        in_specs=[blk, blk, blk,
                  pl.BlockSpec(bias.shape, lambda b, h: (0, 0, 0), pipeline_mode=pl.Buffered(1))],
        out_specs=pl.BlockSpec((None, R * I, HEAD_DIM), lambda b, h: (b, 0, h)),
        out_shape=jax.ShapeDtypeStruct((B, R * I, W), BF16),
        scratch_shapes=[acc, acc, acc, pltpu.VMEM((ATT_TILE, HEAD_DIM), F32)],
        compiler_params=_params("arbitrary", "arbitrary"),
        name="att_rm",
    )(q, k, v, bias)


FFN_SUB = 512
FFN_CHUNK = 256


def _mix_ffn_kernel(x_ref, a1_ref, a2_ref, wm_ref, wi_ref, wo_ref, g_ref, b_ref, o_ref, *, tm):
    W = HALF_WIDTH
    nsub = tm // FFN_SUB
    h1 = [None] * nsub
    hb = [None] * nsub

    def prologue(i):
        rows = slice(i * FFN_SUB, (i + 1) * FFN_SUB)
        mix = _dot(a1_ref[rows, :], wm_ref[0:W, :]) + _dot(a2_ref[rows, :], wm_ref[W:2 * W, :])
        h1[i] = _layer_norm(DEEPNORM_ALPHA * x_ref[rows, :] + mix, g_ref[0:1, :], b_ref[0:1, :])
        hb[i] = h1[i].astype(BF16)

    def epilogue(i, acc):
        rows = slice(i * FFN_SUB, (i + 1) * FFN_SUB)
        o_ref[rows, :] = _layer_norm(DEEPNORM_ALPHA * h1[i] + acc, g_ref[1:2, :], b_ref[1:2, :])

    prologue(0)
    prev_acc = None
    for i in range(nsub):
        acc = None
        for ci, c0 in enumerate(range(0, D_FF, FFN_CHUNK)):
            if ci == 1 and i + 1 < nsub:
                prologue(i + 1)
            if ci == 1 and i > 0:
                epilogue(i - 1, prev_acc)
            gate = _dot(hb[i], wi_ref[:, c0:c0 + FFN_CHUNK])
            up = _dot(hb[i], wi_ref[:, D_FF + c0:D_FF + c0 + FFN_CHUNK])
            part = _dot((_silu(gate) * up).astype(BF16), wo_ref[c0:c0 + FFN_CHUNK, :])
            acc = part if acc is None else acc + part
        prev_acc = acc
    epilogue(nsub - 1, prev_acc)


def _mix_ffn(x2d, a1, a2, w_mix_bf16, w_in_bf16, w_out_bf16, g2, b2):
    T, D = x2d.shape
    tm = 1024
    rowblk = lambda w: pl.BlockSpec((tm, w), lambda i: (i, 0))
    resident = lambda shape: pl.BlockSpec(shape, lambda i: (0, 0), pipeline_mode=pl.Buffered(1))
    return pl.pallas_call(
        functools.partial(_mix_ffn_kernel, tm=tm),
        grid=(T // tm,),
        in_specs=[rowblk(D), rowblk(HALF_WIDTH), rowblk(HALF_WIDTH), resident((D, D)),
                  resident((D, 2 * D_FF)), resident((D_FF, D)), resident((2, D)), resident((2, D))],
        out_specs=rowblk(D),
        out_shape=jax.ShapeDtypeStruct((T, D), F32),
        compiler_params=_params("arbitrary"),
        name="mix_ffn",
    )(x2d, a1, a2, w_mix_bf16, w_in_bf16, w_out_bf16, g2, b2)


def kernel(x, positions, ev_w_in, ev_ret_norm_g, ev_pool_w, ev_pool_scale, ev_w_out, od_w_in, od_conv_w, od_conv_b, od_gate_a_w, od_gate_a_b, od_gate_x_w, od_gate_x_b, od_lru_lambda, od_w_out, ffn_w_in, ffn_w_out, ln_g, ln_b):
    B, S, D = x.shape
    T = B * S
    cos, sin = _rope_tables(positions)
    h = x.reshape(T, D)
    for layer in range(DEPTH):
        j = layer // 2
        h3 = h.reshape(B, S, D)
        if layer % 2 == 0:
            a1, a2 = _even_core(h3, ev_w_in[j].astype(BF16), cos, sin, ev_ret_norm_g[j],
                                ev_pool_w[j].astype(BF16), ev_pool_scale[j])
            w_mix = ev_w_out[j]
        else:
            w_in = od_w_in[j].astype(BF16)
            a1 = _lru_core(h3, w_in[:, :2 * HALF_WIDTH], od_conv_w[j], od_conv_b[j], od_gate_a_w[j], od_gate_a_b[j],
                           od_gate_x_w[j], od_gate_x_b[j], od_lru_lambda[j])
            a2 = _att_rm(*_qkv_rm(h3, w_in[:, 2 * HALF_WIDTH:], cos, sin))
            w_mix = od_w_out[j]
        h = _mix_ffn(h, a1.reshape(T, HALF_WIDTH), a2.reshape(T, HALF_WIDTH), w_mix.astype(BF16),
                     ffn_w_in[layer].astype(BF16), ffn_w_out[layer].astype(BF16), ln_g[layer], ln_b[layer])
    return h.reshape(B, S, D)
```

```python
import functools
import math

import jax
import jax.numpy as jnp
import numpy as np
from jax import lax
from jax.experimental import pallas as pl
from jax.experimental.pallas import tpu as pltpu

F32 = jnp.float32
BF16 = jnp.bfloat16

D_MODEL = 1024
DEPTH = 4
HEAD_DIM = 128
HALF_WIDTH = 512
N_HEADS = HALF_WIDTH // HEAD_DIM
RET_CHUNK = 128
POOL_WINDOWS = (2, 4, 8, 16)
POOL_GROUP = 128
POOL_HALO = 32
LRU_BLOCKS = 8
LRU_C = 8.0
CONV_WIDTH = 4
CONV_HALO = 8
DIL_PATTERNS = ((128, 1), (512, 4), (2048, 16))
ATT_BLOCK = 128
ATT_TILE = 2048
ATT_RES = 16
ROPE_THETA = 10000.0
EVEN_IN = 5 * HALF_WIDTH
ODD_IN = 5 * HALF_WIDTH
D_FF = 2816
DEEPNORM_ALPHA = (2 * DEPTH) ** 0.25
LN_EPS = 1e-5

VMEM_LIMIT = 56 * 1024 * 1024


def _params(*sem):
    return pltpu.CompilerParams(dimension_semantics=sem, vmem_limit_bytes=VMEM_LIMIT)


def _sigmoid(x):
    return 1.0 / (1.0 + jnp.exp(-x))


def _silu(x):
    return x * _sigmoid(x)


def _gelu_tanh(x):
    c = math.sqrt(2.0 / math.pi)
    return x * (0.5 * (1.0 + jnp.tanh(c * (x + 0.044715 * (x * x * x)))))


def _layer_norm(y, g, b):
    mu = jnp.mean(y, axis=-1, keepdims=True)
    d = y - mu
    var = jnp.mean(d * d, axis=-1, keepdims=True)
    return d * lax.rsqrt(var + LN_EPS) * g + b


def _rope(t, cos, sin_signed):
    return t * cos + pltpu.roll(t, HEAD_DIM // 2, axis=1) * sin_signed


def _dot(a, b):
    return jnp.dot(a, b, preferred_element_type=F32)


def _dot_nt(a, b):
    return lax.dot_general(a, b, (((1,), (1,)), ((), ())), preferred_element_type=F32)


def _rope_table_kernel(pos_ref, inv_ref, sign_ref, cos_ref, sin_ref):
    ang = pos_ref[...].astype(F32) * inv_ref[...]
    cos_ref[...] = jnp.cos(ang)
    sin_ref[...] = jnp.sin(ang) * sign_ref[...]


def _rope_tables(positions):
    B, S = positions.shape
    T = B * S
    ts = 2048
    half = HEAD_DIM // 2
    inv = ROPE_THETA ** (-jnp.arange(0, HEAD_DIM, 2, dtype=F32) / HEAD_DIM)
    inv2 = jnp.concatenate([inv, inv])[None, :]
    sign = jnp.concatenate([-jnp.ones((half,), F32), jnp.ones((half,), F32)])[None, :]
    row = pl.BlockSpec((1, HEAD_DIM), lambda i: (0, 0))
    cos, sin = pl.pallas_call(
        _rope_table_kernel,
        grid=(T // ts,),
        in_specs=[pl.BlockSpec((ts, 1), lambda i: (i, 0)), row, row],
        out_specs=[pl.BlockSpec((ts, HEAD_DIM), lambda i: (i, 0))] * 2,
        out_shape=[jax.ShapeDtypeStruct((T, HEAD_DIM), F32)] * 2,
        compiler_params=_params("arbitrary"),
        name="rope_tables",
    )(positions.reshape(T, 1), inv2, sign)
    return cos.reshape(B, S, HEAD_DIM), sin.reshape(B, S, HEAD_DIM)


def _seq_specs(B, nblk, ts):
    last = B * nblk - 1
    cur = lambda w: pl.BlockSpec((None, ts, w), lambda s: (s // nblk, s % nblk, 0))
    first = lambda w: pl.BlockSpec((None, ts, w), lambda s: (0, 0, 0))

    def ahead(w):
        def index_map(s):
            n = jnp.minimum(s + 1, last)
            return (n // nblk, n % nblk, 0)
        return pl.BlockSpec((None, ts, w), index_map)

    return cur, ahead, first


def _project_ahead(step, x0_ref, xn_ref, w_ref, xb_ref, za_ref, zb_ref, body):
    @pl.when(step == 0)
    def _():
        za_ref[...] = _dot(x0_ref[...].astype(BF16), w_ref[...])

    def run(z_cur_ref, z_next_ref):
        def project_piece(k, n):
            if k == 0:
                xb_ref[...] = xn_ref[...].astype(BF16)
            width = w_ref.shape[1] // n
            cols = slice(k * width, (k + 1) * width)
            z_next_ref[:, cols] = _dot(xb_ref[...], w_ref[:, cols])
        body(z_cur_ref, project_piece)

    @pl.when(step % 2 == 0)
    def _():
        run(za_ref, zb_ref)

    @pl.when(step % 2 == 1)
    def _():
        run(zb_ref, za_ref)


def _even_kernel(x0_ref, xn_ref, wi_ref, cos_ref, sin_ref, decay_ref, qdec_ref, kdec_ref, cdec_ref, gn_ref, pw_ref,
                 ps_ref, ret_ref, pool_ref, xb_ref, za_ref, zb_ref, state_ref, pbuf_ref, s2_ref, s4_ref, s8_ref,
                 *, ts, nblk):
    step = pl.program_id(0)
    sblk = step % nblk
    H = POOL_HALO

    @pl.when(sblk == 0)
    def _():
        state_ref[...] = jnp.zeros_like(state_ref)
        pbuf_ref[0:H, :] = jnp.zeros((H, HALF_WIDTH), F32)

    body = functools.partial(_even_body, cos_ref, sin_ref, decay_ref, qdec_ref, kdec_ref, cdec_ref, gn_ref, pw_ref,
                             ps_ref, ret_ref, pool_ref, state_ref, pbuf_ref, s2_ref, s4_ref, s8_ref, sblk, ts)
    _project_ahead(step, x0_ref, xn_ref, wi_ref, xb_ref, za_ref, zb_ref, body)


def _block_diag2(a, b):
    za = jnp.zeros_like(a)
    zb = jnp.zeros_like(b)
    return jnp.concatenate([jnp.concatenate([a, zb], axis=1), jnp.concatenate([za, b], axis=1)], axis=0)


def _pair(ref, hs):
    return jnp.concatenate([ref[h] for h in hs], axis=1)


def _even_body(cos_ref, sin_ref, decay_ref, qdec_ref, kdec_ref, cdec_ref, gn_ref, pw_ref, ps_ref, ret_ref, pool_ref,
               state_ref, pbuf_ref, s2_ref, s4_ref, s8_ref, sblk, ts, z_ref, project_piece):
    H = POOL_HALO
    npieces = ts // RET_CHUNK + 1
    for c in range(ts // RET_CHUNK):
        project_piece(c, npieces)
        rows = pl.ds(c * RET_CHUNK, RET_CHUNK)
        cos = cos_ref[rows, :]
        sin = sin_ref[rows, :]
        for h0 in range(0, N_HEADS, 2):
            hs = (h0, h0 + 1)
            q = [_rope(z_ref[rows, pl.ds(h * HEAD_DIM, HEAD_DIM)], cos, sin) for h in hs]
            k = [_rope(z_ref[rows, pl.ds(HALF_WIDTH + h * HEAD_DIM, HEAD_DIM)], cos, sin) * (HEAD_DIM ** -0.5)
                 for h in hs]
            vb = [z_ref[rows, pl.ds(2 * HALF_WIDTH + h * HEAD_DIM, HEAD_DIM)].astype(BF16) for h in hs]
            state = [state_ref[h] for h in hs]
            qb = jnp.concatenate([t.astype(BF16) for t in q], axis=1)
            scores = _dot_nt(qb, _block_diag2(*[t.astype(BF16) for t in k])) * _pair(decay_ref, hs)
            vbd = _block_diag2(*vb)
            inner = _dot(scores.astype(BF16), vbd)
            cross = _dot(qb, _block_diag2(*[s.astype(BF16) for s in state])) * _pair(qdec_ref, hs)
            kd = jnp.concatenate([(k[i] * kdec_ref[h]).T.astype(BF16) for i, h in enumerate(hs)], axis=1)
            kv = _dot(kd, vbd)
            y2 = inner + cross
            for i, h in enumerate(hs):
                lanes = slice(h * HEAD_DIM, (h + 1) * HEAD_DIM)
                half = slice(i * HEAD_DIM, (i + 1) * HEAD_DIM)
                state_ref[h] = state[i] * cdec_ref[h] + kv[:, half]
                y = y2[:, half]
                gate = z_ref[rows, pl.ds(3 * HALF_WIDTH + h * HEAD_DIM, HEAD_DIM)]
                mu = jnp.mean(y, axis=-1, keepdims=True)
                d = y - mu
                var = jnp.mean(d * d, axis=-1, keepdims=True)
                yn = d * lax.rsqrt(var + LN_EPS) * gn_ref[:, lanes]
                ret_ref[rows, lanes] = (yn * _silu(gate)).astype(ret_ref.dtype)

    project_piece(npieces - 1, npieces)
    pbuf_ref[H:H + ts, :] = z_ref[:, pl.ds(4 * HALF_WIDTH, HALF_WIDTH)]
    G = POOL_GROUP
    s2_ref[8:H + ts, :] = pbuf_ref[8:H + ts, :] + pbuf_ref[7:H + ts - 1, :]
    s4_ref[16:H + ts, :] = s2_ref[16:H + ts, G:4 * G] + s2_ref[14:H + ts - 2, G:4 * G]
    s8_ref[24:H + ts, :] = s4_ref[24:H + ts, G:3 * G] + s4_ref[20:H + ts - 4, G:3 * G]
    s16 = s8_ref[32:H + ts, G:2 * G] + s8_ref[24:H + ts - 8, G:2 * G]
    wsums = (s2_ref[H:H + ts, 0:G], s4_ref[H:H + ts, 0:G], s8_ref[H:H + ts, 0:G], s16)
    tpos = sblk * ts + lax.broadcasted_iota(jnp.int32, (ts, 1), 0) + 1
    for gi, w in enumerate(POOL_WINDOWS):
        lanes = slice(gi * G, (gi + 1) * G)
        cnt = jnp.minimum(tpos, w).astype(F32)
        pooled = wsums[gi] / cnt - pbuf_ref[H:H + ts, lanes]
        mixed = _dot(pooled.astype(BF16), pw_ref[gi]) * ps_ref[:, lanes]
        pool_ref[:, lanes] = mixed.astype(pool_ref.dtype)
    pbuf_ref[0:H, :] = pbuf_ref[ts:ts + H, :]


def _even_core(x3, w_in_bf16, cos, sin, ret_norm_g, pool_w_bf16, pool_scale):
    B, S, D = x3.shape
    ts = 512
    C = RET_CHUNK
    lg = jnp.log1p(-(2.0 ** (-5.0 - jnp.arange(N_HEADS, dtype=F32))))
    idx = jnp.arange(C, dtype=F32)
    rel = idx[:, None] - idx[None, :]
    decay = jnp.where(rel[None] >= 0, jnp.exp(jnp.maximum(rel, 0.0)[None] * lg[:, None, None]), 0.0)
    k_decay = jnp.exp((C - 1 - idx)[None, :] * lg[:, None])
    q_decay = jnp.exp((idx + 1.0)[None, :] * lg[:, None])
    chunk_decay = jnp.exp(C * lg)
    qdec = jnp.broadcast_to(q_decay[:, :, None], (N_HEADS, C, HEAD_DIM))
    kdec = jnp.broadcast_to(k_decay[:, :, None], (N_HEADS, C, HEAD_DIM))
    cdec = jnp.broadcast_to(chunk_decay[:, None, None], (N_HEADS, 1, HEAD_DIM))

    const3 = lambda shape: pl.BlockSpec(shape, lambda s: (0, 0, 0), pipeline_mode=pl.Buffered(1))
    const2 = lambda shape: pl.BlockSpec(shape, lambda s: (0, 0), pipeline_mode=pl.Buffered(1))
    nblk = S // ts
    seq, ahead, first = _seq_specs(B, nblk, ts)
    return pl.pallas_call(
        functools.partial(_even_kernel, ts=ts, nblk=nblk),
        grid=(B * nblk,),
        in_specs=[first(D), ahead(D), const2((D, EVEN_IN)), seq(HEAD_DIM), seq(HEAD_DIM),
                  const3((N_HEADS, C, C)), const3((N_HEADS, C, HEAD_DIM)), const3((N_HEADS, C, HEAD_DIM)),
                  const3((N_HEADS, 1, HEAD_DIM)), const2((1, HALF_WIDTH)),
                  const3((len(POOL_WINDOWS), POOL_GROUP, POOL_GROUP)), const2((1, HALF_WIDTH))],
        out_specs=[seq(HALF_WIDTH), seq(HALF_WIDTH)],
        out_shape=[jax.ShapeDtypeStruct((B, S, HALF_WIDTH), BF16)] * 2,
        scratch_shapes=[pltpu.VMEM((ts, D), BF16), pltpu.VMEM((ts, EVEN_IN), F32), pltpu.VMEM((ts, EVEN_IN), F32),
                        pltpu.VMEM((N_HEADS, HEAD_DIM, HEAD_DIM), F32),
                        pltpu.VMEM((POOL_HALO + ts, HALF_WIDTH), F32),
                        pltpu.VMEM((POOL_HALO + ts, HALF_WIDTH), F32),
                        pltpu.VMEM((POOL_HALO + ts, 3 * POOL_GROUP), F32),
                        pltpu.VMEM((POOL_HALO + ts, 2 * POOL_GROUP), F32)],
        compiler_params=_params("arbitrary"),
        name="even_core",
    )(x3, x3, w_in_bf16, cos, sin, decay, qdec, kdec, cdec, ret_norm_g[None, :], pool_w_bf16, pool_scale[None, :])


def _lru_kernel(x_ref, wi_ref, cw_ref, cb_ref, wa_ref, ba_ref, wx_ref, bx_ref, lam_ref, y_ref,
                z_ref, ubuf_ref, hcar_ref, h_ref, *, ts, nblk):
    step = pl.program_id(0)
    H = CONV_HALO

    @pl.when(step % nblk == 0)
    def _():
        ubuf_ref[0:H, :] = jnp.zeros((H, HALF_WIDTH), F32)
        hcar_ref[...] = jnp.zeros_like(hcar_ref)

    z_ref[...] = _dot(x_ref[...].astype(BF16), wi_ref[...])
    ubuf_ref[H:H + ts, :] = z_ref[:, pl.ds(HALF_WIDTH, HALF_WIDTH)]
    u = cb_ref[...] + cw_ref[CONV_WIDTH - 1:CONV_WIDTH, :] * ubuf_ref[H:H + ts, :]
    for k in range(CONV_WIDTH - 1):
        off = H - (CONV_WIDTH - 1) + k
        u = u + cw_ref[k:k + 1, :] * ubuf_ref[off:off + ts, :]
    ubuf_ref[0:H, :] = ubuf_ref[ts:ts + H, :]

    ub = u.astype(BF16)
    r = _sigmoid(_dot(ub, wa_ref[...]) + ba_ref[...])
    ig = _sigmoid(_dot(ub, wx_ref[...]) + bx_ref[...])
    nl = -lam_ref[...]
    softplus = jnp.maximum(nl, 0.0) + jnp.log1p(jnp.exp(-jnp.abs(nl)))
    log_a = (-LRU_C * r) * softplus
    a = jnp.exp(log_a)
    bseq = jnp.sqrt(jnp.tanh(-log_a) * (1.0 + a * a)) * (ig * u)

    G = 8
    a = a.reshape(ts // G, G, HALF_WIDTH)
    bseq = bseq.reshape(ts // G, G, HALF_WIDTH)
    row = lax.broadcasted_iota(jnp.int32, (1, G, 1), 1)
    sh = 1
    while sh < G:
        a_prev = pltpu.roll(a, sh, axis=1)
        b_prev = pltpu.roll(bseq, sh, axis=1)
        m = row >= sh
        bseq = jnp.where(m, a * b_prev + bseq, bseq)
        a = jnp.where(m, a * a_prev, a)
        sh *= 2
    carry = hcar_ref[0:1, :]
    for v in range(ts // G):
        hv = bseq[v] + a[v] * carry
        h_ref[G * v:G * (v + 1), :] = hv
        carry = hv[G - 1:G]
    hcar_ref[...] = jnp.broadcast_to(carry, hcar_ref.shape)
    y_ref[...] = (h_ref[...] * _gelu_tanh(z_ref[:, 0:HALF_WIDTH])).astype(y_ref.dtype)


def _block_diag(w):
    n, c, d = w.shape
    eye = jnp.eye(n, dtype=w.dtype)
    return (w[:, :, None, :] * eye[:, None, :, None]).reshape(n * c, n * d)


def _lru_core(x3, w_in_bf16, conv_w, conv_b, gate_a_w, gate_a_b, gate_x_w, gate_x_b, lam):
    B, S, D = x3.shape
    ts = 512
    W = HALF_WIDTH
    wa = _block_diag(gate_a_w).astype(BF16)
    wx = _block_diag(gate_x_w).astype(BF16)
    const = lambda shape: pl.BlockSpec(shape, lambda s: (0, 0), pipeline_mode=pl.Buffered(1))
    nblk = S // ts
    seq, _, _ = _seq_specs(B, nblk, ts)
    return pl.pallas_call(
        functools.partial(_lru_kernel, ts=ts, nblk=nblk),
        grid=(B * nblk,),
        in_specs=[seq(D), const((D, 2 * W)),
                  const((CONV_WIDTH, W)), const((1, W)), const((W, W)), const((1, W)),
                  const((W, W)), const((1, W)), const((1, W))],
        out_specs=seq(W),
        out_shape=jax.ShapeDtypeStruct((B, S, W), BF16),
        scratch_shapes=[pltpu.VMEM((ts, 2 * W), F32),
                        pltpu.VMEM((CONV_HALO + ts, W), F32), pltpu.VMEM((8, W), F32), pltpu.VMEM((ts, W), F32)],
        compiler_params=_params("arbitrary"),
        name="lru_core",
    )(x3, w_in_bf16, conv_w, conv_b[None, :], wa, gate_a_b[None, :], wx, gate_x_b[None, :], lam[None, :])


def _qkv_rm_kernel(x_ref, w_ref, cos_ref, sin_ref, q_ref, k_ref, v_ref, xs_ref):
    R = ATT_RES
    rows_per = x_ref.shape[0] // R
    ncol = x_ref.shape[1] // HEAD_DIM

    def residue_rows(ref, r):
        return ref[pl.ds(r, rows_per, stride=R), :]

    for c in range(ncol):
        xs_ref[c] = x_ref[:, c * HEAD_DIM:(c + 1) * HEAD_DIM]
    qscale = HEAD_DIM ** -0.5 * math.log2(math.e)
    half = R // 2
    for r0 in (0, half):
        x = jnp.concatenate(
            [jnp.concatenate([residue_rows(xs_ref.at[c], r) for c in range(ncol)], axis=1)
             for r in range(r0, r0 + half)], axis=0).astype(BF16)
        z = _dot(x, w_ref[...])
        for r in range(r0, r0 + half):
            rows = slice((r - r0) * rows_per, (r - r0 + 1) * rows_per)
            cos = residue_rows(cos_ref, r)
            sin = residue_rows(sin_ref, r)
            for h in range(N_HEADS):
                lanes = slice(h * HEAD_DIM, (h + 1) * HEAD_DIM)
                q_ref[r, :, lanes] = _rope(z[rows, h * HEAD_DIM:(h + 1) * HEAD_DIM], cos, sin) * qscale
                k_ref[r, :, lanes] = _rope(z[rows, HALF_WIDTH + h * HEAD_DIM:HALF_WIDTH + (h + 1) * HEAD_DIM],
                                           cos, sin)
            v_ref[r] = z[rows, 2 * HALF_WIDTH:3 * HALF_WIDTH]


def _qkv_rm(x3, w_bf16, cos, sin):
    B, S, D = x3.shape
    R = ATT_RES
    I = S // R
    W = HALF_WIDTH
    ts = 1024
    seq = lambda w: pl.BlockSpec((None, ts, w), lambda b, t: (b, t, 0))
    out_spec = pl.BlockSpec((None, R, ts // R, W), lambda b, t: (b, 0, t, 0))
    return pl.pallas_call(
        _qkv_rm_kernel,
        grid=(B, S // ts),
        in_specs=[seq(D), pl.BlockSpec((D, 3 * W), lambda b, t: (0, 0), pipeline_mode=pl.Buffered(1)),
                  seq(HEAD_DIM), seq(HEAD_DIM)],
        out_specs=[out_spec] * 3,
        out_shape=[jax.ShapeDtypeStruct((B, R, I, W), F32)] * 3,
        scratch_shapes=[pltpu.VMEM((D // HEAD_DIM, ts, HEAD_DIM), F32)],
        compiler_params=_params("arbitrary", "arbitrary"),
        name="qkv_rm",
    )(x3, w_bf16, cos, sin)


def _pattern_bias(dil):
    nslab = ATT_RES // dil
    slab = ATT_BLOCK // nslab
    row = np.arange(ATT_BLOCK)
    col = np.arange(2 * ATT_BLOCK)
    jq = nslab * (row % slab) + row // slab + ATT_BLOCK
    jk = nslab * (col % slab) + (col % ATT_BLOCK) // slab + ATT_BLOCK * (col // ATT_BLOCK)
    delta = jq[:, None] - jk[None, :]
    return np.where((delta >= 0) & (delta <= ATT_BLOCK), 0.0, -np.inf).astype(np.float32)


def _att_rm_kernel(q_ref, k_ref, v_ref, bias_ref, o_ref, num_ref, den_ref, mx_ref, onat_ref):
    QB = ATT_BLOCK
    TI = ATT_TILE // ATT_RES
    ones = jnp.ones((2 * QB, HEAD_DIM), BF16)

    def gather(ref, slabs, r0, rn):
        return jnp.concatenate([ref[r, r0:r0 + rn, :] for r in slabs], axis=0)

    for tile in range(q_ref.shape[1] // TI):
        i0 = tile * TI
        for p, (window, dil) in enumerate(DIL_PATTERNS):
            assert window // dil == QB
            nslab = ATT_RES // dil
            slab = QB // nslab
            for n in range(ATT_TILE // QB):
                res, m = n % dil, n // dil
                slabs = [res + dil * a for a in range(nslab)]
                c0 = i0 + slab * m
                qb = gather(q_ref, slabs, c0, slab).astype(BF16)
                if c0 == 0:
                    kb = gather(k_ref, slabs, c0, slab).astype(BF16)
                    vb = gather(v_ref, slabs, c0, slab).astype(BF16)
                    bias = bias_ref[p, :, QB:2 * QB]
                else:
                    kb = jnp.concatenate([gather(k_ref, slabs, c0 - slab, slab),
                                          gather(k_ref, slabs, c0, slab)], axis=0).astype(BF16)
                    vb = jnp.concatenate([gather(v_ref, slabs, c0 - slab, slab),
                                          gather(v_ref, slabs, c0, slab)], axis=0).astype(BF16)
                    bias = bias_ref[p]
                s = _dot_nt(qb, kb) + bias
                mx = jnp.max(s, axis=-1, keepdims=True)
                e = jnp.exp2(s - mx).astype(BF16)
                nd = _dot(e, jnp.concatenate([vb, ones[0:vb.shape[0]]], axis=1))
                mxb = jnp.broadcast_to(mx, (QB, HEAD_DIM))
                for a, r in enumerate(slabs):
                    rows = slice(a * slab, (a + 1) * slab)
                    dst = slice(slab * m, slab * (m + 1))
                    num_ref[p, r, dst, :] = nd[rows, 0:HEAD_DIM]
                    den_ref[p, r, dst, :] = nd[rows, HEAD_DIM:2 * HEAD_DIM]
                    mx_ref[p, r, dst, :] = mxb[rows]

        m0, m1, m2 = mx_ref[0], mx_ref[1], mx_ref[2]
        mm = jnp.maximum(jnp.maximum(m0, m1), m2)
        w0, w1, w2 = jnp.exp2(m0 - mm), jnp.exp2(m1 - mm), jnp.exp2(m2 - mm)
        num = w0 * num_ref[0] + w1 * num_ref[1] + w2 * num_ref[2]
        den = w0 * den_ref[0] + w1 * den_ref[1] + w2 * den_ref[2]
        out = num / den
        for r in range(ATT_RES):
            onat_ref[pl.ds(r, TI, stride=ATT_RES), :] = out[r]
        o_ref[tile * ATT_TILE:(tile + 1) * ATT_TILE, :] = onat_ref[...].astype(o_ref.dtype)


def _att_rm(q, k, v):
    B, R, I, W = q.shape
    TI = ATT_TILE // ATT_RES
    assert R == ATT_RES and I % TI == 0
    bias = jnp.asarray(np.stack([_pattern_bias(dil) for _, dil in DIL_PATTERNS]))
    blk = pl.BlockSpec((None, R, I, HEAD_DIM), lambda b, h: (b, 0, 0, h))
    acc = pltpu.VMEM((len(DIL_PATTERNS), R, TI, HEAD_DIM), F32)
    return pl.pallas_call(
        _att_rm_kernel,
        grid=(B, N_HEADS),
        in_specs=[blk, blk, blk,
                  pl.BlockSpec(bias.shape, lambda b, h: (0, 0, 0), pipeline_mode=pl.Buffered(1))],
        out_specs=pl.BlockSpec((None, R * I, HEAD_DIM), lambda b, h: (b, 0, h)),
        out_shape=jax.ShapeDtypeStruct((B, R * I, W), BF16),
        scratch_shapes=[acc, acc, acc, pltpu.VMEM((ATT_TILE, HEAD_DIM), F32)],
        compiler_params=_params("arbitrary", "arbitrary"),
        name="att_rm",
    )(q, k, v, bias)


FFN_SUB = 512
FFN_CHUNK = 256


def _mix_ffn_kernel(x_ref, a1_ref, a2_ref, wm_ref, wi_ref, wo_ref, g_ref, b_ref, o_ref, *, tm):
    W = HALF_WIDTH
    nsub = tm // FFN_SUB
    h1 = [None] * nsub
    hb = [None] * nsub

    def prologue(i):
        rows = slice(i * FFN_SUB, (i + 1) * FFN_SUB)
        mix = _dot(a1_ref[rows, :], wm_ref[0:W, :]) + _dot(a2_ref[rows, :], wm_ref[W:2 * W, :])
        h1[i] = _layer_norm(DEEPNORM_ALPHA * x_ref[rows, :] + mix, g_ref[0:1, :], b_ref[0:1, :])
        hb[i] = h1[i].astype(BF16)

    def epilogue(i, acc):
        rows = slice(i * FFN_SUB, (i + 1) * FFN_SUB)
        o_ref[rows, :] = _layer_norm(DEEPNORM_ALPHA * h1[i] + acc, g_ref[1:2, :], b_ref[1:2, :])

    prologue(0)
    prev_acc = None
    for i in range(nsub):
        acc = None
        for ci, c0 in enumerate(range(0, D_FF, FFN_CHUNK)):
            if ci == 1 and i + 1 < nsub:
                prologue(i + 1)
            if ci == 1 and i > 0:
                epilogue(i - 1, prev_acc)
            gate = _dot(hb[i], wi_ref[:, c0:c0 + FFN_CHUNK])
            up = _dot(hb[i], wi_ref[:, D_FF + c0:D_FF + c0 + FFN_CHUNK])
            part = _dot((_silu(gate) * up).astype(BF16), wo_ref[c0:c0 + FFN_CHUNK, :])
            acc = part if acc is None else acc + part
        prev_acc = acc
    epilogue(nsub - 1, prev_acc)


def _mix_ffn(x2d, a1, a2, w_mix_bf16, w_in_bf16, w_out_bf16, g2, b2):
    T, D = x2d.shape
    tm = 1024
    rowblk = lambda w: pl.BlockSpec((tm, w), lambda i: (i, 0))
    resident = lambda shape: pl.BlockSpec(shape, lambda i: (0, 0), pipeline_mode=pl.Buffered(1))
    return pl.pallas_call(
        functools.partial(_mix_ffn_kernel, tm=tm),
        grid=(T // tm,),
        in_specs=[rowblk(D), rowblk(HALF_WIDTH), rowblk(HALF_WIDTH), resident((D, D)),
                  resident((D, 2 * D_FF)), resident((D_FF, D)), resident((2, D)), resident((2, D))],
        out_specs=rowblk(D),
        out_shape=jax.ShapeDtypeStruct((T, D), F32),
        compiler_params=_params("arbitrary"),
        name="mix_ffn",
    )(x2d, a1, a2, w_mix_bf16, w_in_bf16, w_out_bf16, g2, b2)


def kernel(x, positions, ev_w_in, ev_ret_norm_g, ev_pool_w, ev_pool_scale, ev_w_out, od_w_in, od_conv_w, od_conv_b, od_gate_a_w, od_gate_a_b, od_gate_x_w, od_gate_x_b, od_lru_lambda, od_w_out, ffn_w_in, ffn_w_out, ln_g, ln_b):
    B, S, D = x.shape
    T = B * S
    cos, sin = _rope_tables(positions)
    h = x.reshape(T, D)
    for layer in range(DEPTH):
        j = layer // 2
        h3 = h.reshape(B, S, D)
        if layer % 2 == 0:
            a1, a2 = _even_core(h3, ev_w_in[j].astype(BF16), cos, sin, ev_ret_norm_g[j],
                                ev_pool_w[j].astype(BF16), ev_pool_scale[j])
            w_mix = ev_w_out[j]
        else:
            w_in = od_w_in[j].astype(BF16)
            a1 = _lru_core(h3, w_in[:, :2 * HALF_WIDTH], od_conv_w[j], od_conv_b[j], od_gate_a_w[j], od_gate_a_b[j],
                           od_gate_x_w[j], od_gate_x_b[j], od_lru_lambda[j])
            a2 = _att_rm(*_qkv_rm(h3, w_in[:, 2 * HALF_WIDTH:], cos, sin))
            w_mix = od_w_out[j]
        h = _mix_ffn(h, a1.reshape(T, HALF_WIDTH), a2.reshape(T, HALF_WIDTH), w_mix.astype(BF16),
                     ffn_w_in[layer].astype(BF16), ffn_w_out[layer].astype(BF16), ln_g[layer], ln_b[layer])
    return h.reshape(B, S, D)
```

```python
import functools
import math

import jax
import jax.numpy as jnp
import numpy as np
from jax import lax
from jax.experimental import pallas as pl
from jax.experimental.pallas import tpu as pltpu

F32 = jnp.float32
BF16 = jnp.bfloat16

D_MODEL = 1024
DEPTH = 4
HEAD_DIM = 128
HALF_WIDTH = 512
N_HEADS = HALF_WIDTH // HEAD_DIM
RET_CHUNK = 128
POOL_WINDOWS = (2, 4, 8, 16)
POOL_GROUP = 128
POOL_HALO = 32
LRU_BLOCKS = 8
LRU_C = 8.0
CONV_WIDTH = 4
CONV_HALO = 8
DIL_PATTERNS = ((128, 1), (512, 4), (2048, 16))
ATT_BLOCK = 128
ATT_TILE = 2048
ATT_RES = 16
ROPE_THETA = 10000.0
EVEN_IN = 5 * HALF_WIDTH
ODD_IN = 5 * HALF_WIDTH
D_FF = 2816
DEEPNORM_ALPHA = (2 * DEPTH) ** 0.25
LN_EPS = 1e-5

VMEM_LIMIT = 56 * 1024 * 1024


def _params(*sem):
    return pltpu.CompilerParams(dimension_semantics=sem, vmem_limit_bytes=VMEM_LIMIT)


def _sigmoid(x):
    return 1.0 / (1.0 + jnp.exp(-x))


def _silu(x):
    return x * _sigmoid(x)


def _gelu_tanh(x):
    c = math.sqrt(2.0 / math.pi)
    return x * (0.5 * (1.0 + jnp.tanh(c * (x + 0.044715 * (x * x * x)))))


def _layer_norm(y, g, b):
    mu = jnp.mean(y, axis=-1, keepdims=True)
    d = y - mu
    var = jnp.mean(d * d, axis=-1, keepdims=True)
    return d * lax.rsqrt(var + LN_EPS) * g + b


def _rope(t, cos, sin_signed):
    return t * cos + pltpu.roll(t, HEAD_DIM // 2, axis=1) * sin_signed


def _dot(a, b):
    return jnp.dot(a, b, preferred_element_type=F32)


def _dot_nt(a, b):
    return lax.dot_general(a, b, (((1,), (1,)), ((), ())), preferred_element_type=F32)


def _rope_table_kernel(pos_ref, inv_ref, sign_ref, cos_ref, sin_ref):
    ang = pos_ref[...].astype(F32) * inv_ref[...]
    cos_ref[...] = jnp.cos(ang)
    sin_ref[...] = jnp.sin(ang) * sign_ref[...]


def _rope_tables(positions):
    B, S = positions.shape
    T = B * S
    ts = 2048
    half = HEAD_DIM // 2
    inv = ROPE_THETA ** (-jnp.arange(0, HEAD_DIM, 2, dtype=F32) / HEAD_DIM)
    inv2 = jnp.concatenate([inv, inv])[None, :]
    sign = jnp.concatenate([-jnp.ones((half,), F32), jnp.ones((half,), F32)])[None, :]
    row = pl.BlockSpec((1, HEAD_DIM), lambda i: (0, 0))
    cos, sin = pl.pallas_call(
        _rope_table_kernel,
        grid=(T // ts,),
        in_specs=[pl.BlockSpec((ts, 1), lambda i: (i, 0)), row, row],
        out_specs=[pl.BlockSpec((ts, HEAD_DIM), lambda i: (i, 0))] * 2,
        out_shape=[jax.ShapeDtypeStruct((T, HEAD_DIM), F32)] * 2,
        compiler_params=_params("arbitrary"),
        name="rope_tables",
    )(positions.reshape(T, 1), inv2, sign)
    return cos.reshape(B, S, HEAD_DIM), sin.reshape(B, S, HEAD_DIM)


def _seq_specs(B, nblk, ts):
    last = B * nblk - 1
    cur = lambda w: pl.BlockSpec((None, ts, w), lambda s: (s // nblk, s % nblk, 0))
    first = lambda w: pl.BlockSpec((None, ts, w), lambda s: (0, 0, 0))

    def ahead(w):
        def index_map(s):
            n = jnp.minimum(s + 1, last)
            return (n // nblk, n % nblk, 0)
        return pl.BlockSpec((None, ts, w), index_map)

    return cur, ahead, first


def _project_ahead(step, x0_ref, xn_ref, w_ref, xb_ref, za_ref, zb_ref, body):
    @pl.when(step == 0)
    def _():
        za_ref[...] = _dot(x0_ref[...].astype(BF16), w_ref[...])

    def run(z_cur_ref, z_next_ref):
        def project_piece(k, n):
            if k == 0:
                xb_ref[...] = xn_ref[...].astype(BF16)
            width = w_ref.shape[1] // n
            cols = slice(k * width, (k + 1) * width)
            z_next_ref[:, cols] = _dot(xb_ref[...], w_ref[:, cols])
        body(z_cur_ref, project_piece)

    @pl.when(step % 2 == 0)
    def _():
        run(za_ref, zb_ref)

    @pl.when(step % 2 == 1)
    def _():
        run(zb_ref, za_ref)


def _even_kernel(x0_ref, xn_ref, wi_ref, cos_ref, sin_ref, decay_ref, qdec_ref, kdec_ref, cdec_ref, gn_ref, pw_ref,
                 ps_ref, ret_ref, pool_ref, xb_ref, za_ref, zb_ref, state_ref, pbuf_ref, s2_ref, s4_ref, s8_ref,
                 *, ts, nblk):
    step = pl.program_id(0)
    sblk = step % nblk
    H = POOL_HALO

    @pl.when(sblk == 0)
    def _():
        state_ref[...] = jnp.zeros_like(state_ref)
        pbuf_ref[0:H, :] = jnp.zeros((H, HALF_WIDTH), F32)

    body = functools.partial(_even_body, cos_ref, sin_ref, decay_ref, qdec_ref, kdec_ref, cdec_ref, gn_ref, pw_ref,
                             ps_ref, ret_ref, pool_ref, state_ref, pbuf_ref, s2_ref, s4_ref, s8_ref, sblk, ts)
    _project_ahead(step, x0_ref, xn_ref, wi_ref, xb_ref, za_ref, zb_ref, body)


def _block_diag2(a, b):
    za = jnp.zeros_like(a)
    zb = jnp.zeros_like(b)
    return jnp.concatenate([jnp.concatenate([a, zb], axis=1), jnp.concatenate([za, b], axis=1)], axis=0)


def _pair(ref, hs):
    return jnp.concatenate([ref[h] for h in hs], axis=1)


def _even_body(cos_ref, sin_ref, decay_ref, qdec_ref, kdec_ref, cdec_ref, gn_ref, pw_ref, ps_ref, ret_ref, pool_ref,
               state_ref, pbuf_ref, s2_ref, s4_ref, s8_ref, sblk, ts, z_ref, project_piece):
    H = POOL_HALO
    npieces = ts // RET_CHUNK + 1
    for c in range(ts // RET_CHUNK):
        project_piece(c, npieces)
        rows = pl.ds(c * RET_CHUNK, RET_CHUNK)
        cos = cos_ref[rows, :]
        sin = sin_ref[rows, :]
        for h0 in range(0, N_HEADS, 2):
            hs = (h0, h0 + 1)
            q = [_rope(z_ref[rows, pl.ds(h * HEAD_DIM, HEAD_DIM)], cos, sin) for h in hs]
            k = [_rope(z_ref[rows, pl.ds(HALF_WIDTH + h * HEAD_DIM, HEAD_DIM)], cos, sin) * (HEAD_DIM ** -0.5)
                 for h in hs]
            vb = [z_ref[rows, pl.ds(2 * HALF_WIDTH + h * HEAD_DIM, HEAD_DIM)].astype(BF16) for h in hs]
            state = [state_ref[h] for h in hs]
            qb = jnp.concatenate([t.astype(BF16) for t in q], axis=1)
            scores = _dot_nt(qb, _block_diag2(*[t.astype(BF16) for t in k])) * _pair(decay_ref, hs)
            vbd = _block_diag2(*vb)
            inner = _dot(scores.astype(BF16), vbd)
            cross = _dot(qb, _block_diag2(*[s.astype(BF16) for s in state])) * _pair(qdec_ref, hs)
            kd = jnp.concatenate([(k[i] * kdec_ref[h]).T.astype(BF16) for i, h in enumerate(hs)], axis=1)
            kv = _dot(kd, vbd)
            y2 = inner + cross
            for i, h in enumerate(hs):
                lanes = slice(h * HEAD_DIM, (h + 1) * HEAD_DIM)
                half = slice(i * HEAD_DIM, (i + 1) * HEAD_DIM)
                state_ref[h] = state[i] * cdec_ref[h] + kv[:, half]
                y = y2[:, half]
                gate = z_ref[rows, pl.ds(3 * HALF_WIDTH + h * HEAD_DIM, HEAD_DIM)]
                mu = jnp.mean(y, axis=-1, keepdims=True)
                d = y - mu
                var = jnp.mean(d * d, axis=-1, keepdims=True)
                yn = d * lax.rsqrt(var + LN_EPS) * gn_ref[:, lanes]
                ret_ref[rows, lanes] = (yn * _silu(gate)).astype(ret_ref.dtype)

    project_piece(npieces - 1, npieces)
    pbuf_ref[H:H + ts, :] = z_ref[:, pl.ds(4 * HALF_WIDTH, HALF_WIDTH)]
    G = POOL_GROUP
    s2_ref[8:H + ts, :] = pbuf_ref[8:H + ts, :] + pbuf_ref[7:H + ts - 1, :]
    s4_ref[16:H + ts, :] = s2_ref[16:H + ts, G:4 * G] + s2_ref[14:H + ts - 2, G:4 * G]
    s8_ref[24:H + ts, :] = s4_ref[24:H + ts, G:3 * G] + s4_ref[20:H + ts - 4, G:3 * G]
    s16 = s8_ref[32:H + ts, G:2 * G] + s8_ref[24:H + ts - 8, G:2 * G]
    wsums = (s2_ref[H:H + ts, 0:G], s4_ref[H:H + ts, 0:G], s8_ref[H:H + ts, 0:G], s16)
    tpos = sblk * ts + lax.broadcasted_iota(jnp.int32, (ts, 1), 0) + 1
    for gi, w in enumerate(POOL_WINDOWS):
        lanes = slice(gi * G, (gi + 1) * G)
        cnt = jnp.minimum(tpos, w).astype(F32)
        pooled = wsums[gi] / cnt - pbuf_ref[H:H + ts, lanes]
        mixed = _dot(pooled.astype(BF16), pw_ref[gi]) * ps_ref[:, lanes]
        pool_ref[:, lanes] = mixed.astype(pool_ref.dtype)
    pbuf_ref[0:H, :] = pbuf_ref[ts:ts + H, :]


def _even_core(x3, w_in_bf16, cos, sin, ret_norm_g, pool_w_bf16, pool_scale):
    B, S, D = x3.shape
    ts = 512
    C = RET_CHUNK
    lg = jnp.log1p(-(2.0 ** (-5.0 - jnp.arange(N_HEADS, dtype=F32))))
    idx = jnp.arange(C, dtype=F32)
    rel = idx[:, None] - idx[None, :]
    decay = jnp.where(rel[None] >= 0, jnp.exp(jnp.maximum(rel, 0.0)[None] * lg[:, None, None]), 0.0)
    k_decay = jnp.exp((C - 1 - idx)[None, :] * lg[:, None])
    q_decay = jnp.exp((idx + 1.0)[None, :] * lg[:, None])
    chunk_decay = jnp.exp(C * lg)
    qdec = jnp.broadcast_to(q_decay[:, :, None], (N_HEADS, C, HEAD_DIM))
    kdec = jnp.broadcast_to(k_decay[:, :, None], (N_HEADS, C, HEAD_DIM))
    cdec = jnp.broadcast_to(chunk_decay[:, None, None], (N_HEADS, 1, HEAD_DIM))

    const3 = lambda shape: pl.BlockSpec(shape, lambda s: (0, 0, 0), pipeline_mode=pl.Buffered(1))
    const2 = lambda shape: pl.BlockSpec(shape, lambda s: (0, 0), pipeline_mode=pl.Buffered(1))
    nblk = S // ts
    seq, ahead, first = _seq_specs(B, nblk, ts)
    return pl.pallas_call(
        functools.partial(_even_kernel, ts=ts, nblk=nblk),
        grid=(B * nblk,),
        in_specs=[first(D), ahead(D), const2((D, EVEN_IN)), seq(HEAD_DIM), seq(HEAD_DIM),
                  const3((N_HEADS, C, C)), const3((N_HEADS, C, HEAD_DIM)), const3((N_HEADS, C, HEAD_DIM)),
                  const3((N_HEADS, 1, HEAD_DIM)), const2((1, HALF_WIDTH)),
                  const3((len(POOL_WINDOWS), POOL_GROUP, POOL_GROUP)), const2((1, HALF_WIDTH))],
        out_specs=[seq(HALF_WIDTH), seq(HALF_WIDTH)],
        out_shape=[jax.ShapeDtypeStruct((B, S, HALF_WIDTH), BF16)] * 2,
        scratch_shapes=[pltpu.VMEM((ts, D), BF16), pltpu.VMEM((ts, EVEN_IN), F32), pltpu.VMEM((ts, EVEN_IN), F32),
                        pltpu.VMEM((N_HEADS, HEAD_DIM, HEAD_DIM), F32),
                        pltpu.VMEM((POOL_HALO + ts, HALF_WIDTH), F32),
                        pltpu.VMEM((POOL_HALO + ts, HALF_WIDTH), F32),
                        pltpu.VMEM((POOL_HALO + ts, 3 * POOL_GROUP), F32),
                        pltpu.VMEM((POOL_HALO + ts, 2 * POOL_GROUP), F32)],
        compiler_params=_params("arbitrary"),
        name="even_core",
    )(x3, x3, w_in_bf16, cos, sin, decay, qdec, kdec, cdec, ret_norm_g[None, :], pool_w_bf16, pool_scale[None, :])


def _lru_kernel(x_ref, wi_ref, cw_ref, cb_ref, wa_ref, ba_ref, wx_ref, bx_ref, lam_ref, y_ref,
                zg_ref, ubuf_ref, hcar_ref, h_ref, *, ts, nblk):
    step = pl.program_id(0)
    H = CONV_HALO

    @pl.when(step % nblk == 0)
    def _():
        ubuf_ref[0:H, :] = jnp.zeros((H, HALF_WIDTH), F32)
        hcar_ref[...] = jnp.zeros_like(hcar_ref)

    xb = x_ref[...].astype(BF16)
    ubuf_ref[H:H + ts, :] = _dot(xb, wi_ref[:, HALF_WIDTH:2 * HALF_WIDTH])
    zg_ref[...] = _dot(xb, wi_ref[:, 0:HALF_WIDTH])
    u = cb_ref[...] + cw_ref[CONV_WIDTH - 1:CONV_WIDTH, :] * ubuf_ref[H:H + ts, :]
    for k in range(CONV_WIDTH - 1):
        off = H - (CONV_WIDTH - 1) + k
        u = u + cw_ref[k:k + 1, :] * ubuf_ref[off:off + ts, :]
    ubuf_ref[0:H, :] = ubuf_ref[ts:ts + H, :]

    ub = u.astype(BF16)
    r = _sigmoid(_dot(ub, wa_ref[...]) + ba_ref[...])
    ig = _sigmoid(_dot(ub, wx_ref[...]) + bx_ref[...])
    nl = -lam_ref[...]
    softplus = jnp.maximum(nl, 0.0) + jnp.log1p(jnp.exp(-jnp.abs(nl)))
    log_a = (-LRU_C * r) * softplus
    a = jnp.exp(log_a)
    bseq = jnp.sqrt(jnp.tanh(-log_a) * (1.0 + a * a)) * (ig * u)

    G = 8
    a = a.reshape(ts // G, G, HALF_WIDTH)
    bseq = bseq.reshape(ts // G, G, HALF_WIDTH)
    row = lax.broadcasted_iota(jnp.int32, (1, G, 1), 1)
    sh = 1
    while sh < G:
        a_prev = pltpu.roll(a, sh, axis=1)
        b_prev = pltpu.roll(bseq, sh, axis=1)
        m = row >= sh
        bseq = jnp.where(m, a * b_prev + bseq, bseq)
        a = jnp.where(m, a * a_prev, a)
        sh *= 2
    carry = hcar_ref[0:1, :]
    for v in range(ts // G):
        hv = bseq[v] + a[v] * carry
        h_ref[G * v:G * (v + 1), :] = hv
        carry = hv[G - 1:G]
    hcar_ref[...] = jnp.broadcast_to(carry, hcar_ref.shape)
    y_ref[...] = (h_ref[...] * _gelu_tanh(zg_ref[...])).astype(y_ref.dtype)


def _block_diag(w):
    n, c, d = w.shape
    eye = jnp.eye(n, dtype=w.dtype)
    return (w[:, :, None, :] * eye[:, None, :, None]).reshape(n * c, n * d)


def _lru_core(x3, w_in_bf16, conv_w, conv_b, gate_a_w, gate_a_b, gate_x_w, gate_x_b, lam):
    B, S, D = x3.shape
    ts = 512
    W = HALF_WIDTH
    wa = _block_diag(gate_a_w).astype(BF16)
    wx = _block_diag(gate_x_w).astype(BF16)
    const = lambda shape: pl.BlockSpec(shape, lambda s: (0, 0), pipeline_mode=pl.Buffered(1))
    nblk = S // ts
    seq, _, _ = _seq_specs(B, nblk, ts)
    return pl.pallas_call(
        functools.partial(_lru_kernel, ts=ts, nblk=nblk),
        grid=(B * nblk,),
        in_specs=[seq(D), const((D, 2 * W)),
                  const((CONV_WIDTH, W)), const((1, W)), const((W, W)), const((1, W)),
                  const((W, W)), const((1, W)), const((1, W))],
        out_specs=seq(W),
        out_shape=jax.ShapeDtypeStruct((B, S, W), BF16),
        scratch_shapes=[pltpu.VMEM((ts, W), F32),
                        pltpu.VMEM((CONV_HALO + ts, W), F32), pltpu.VMEM((8, W), F32), pltpu.VMEM((ts, W), F32)],
        compiler_params=_params("arbitrary"),
        name="lru_core",
    )(x3, w_in_bf16, conv_w, conv_b[None, :], wa, gate_a_b[None, :], wx, gate_x_b[None, :], lam[None, :])


def _qkv_rm_kernel(x_ref, w_ref, cos_ref, sin_ref, q_ref, k_ref, v_ref, xs_ref):
    R = ATT_RES
    rows_per = x_ref.shape[0] // R
    ncol = x_ref.shape[1] // HEAD_DIM

    def residue_rows(ref, r):
        return ref[pl.ds(r, rows_per, stride=R), :]

    for c in range(ncol):
        xs_ref[c] = x_ref[:, c * HEAD_DIM:(c + 1) * HEAD_DIM]
    qscale = HEAD_DIM ** -0.5 * math.log2(math.e)
    half = R // 2
    for r0 in (0, half):
        x = jnp.concatenate(
            [jnp.concatenate([residue_rows(xs_ref.at[c], r) for c in range(ncol)], axis=1)
             for r in range(r0, r0 + half)], axis=0).astype(BF16)
        z = _dot(x, w_ref[...])
        for r in range(r0, r0 + half):
            rows = slice((r - r0) * rows_per, (r - r0 + 1) * rows_per)
            cos = residue_rows(cos_ref, r)
            sin = residue_rows(sin_ref, r)
            for h in range(N_HEADS):
                lanes = slice(h * HEAD_DIM, (h + 1) * HEAD_DIM)
                q_ref[r, :, lanes] = _rope(z[rows, h * HEAD_DIM:(h + 1) * HEAD_DIM], cos, sin) * qscale
                k_ref[r, :, lanes] = _rope(z[rows, HALF_WIDTH + h * HEAD_DIM:HALF_WIDTH + (h + 1) * HEAD_DIM],
                                           cos, sin)
            v_ref[r] = z[rows, 2 * HALF_WIDTH:3 * HALF_WIDTH]


def _qkv_rm(x3, w_bf16, cos, sin):
    B, S, D = x3.shape
    R = ATT_RES
    I = S // R
    W = HALF_WIDTH
    ts = 1024
    seq = lambda w: pl.BlockSpec((None, ts, w), lambda b, t: (b, t, 0))
    out_spec = pl.BlockSpec((None, R, ts // R, W), lambda b, t: (b, 0, t, 0))
    return pl.pallas_call(
        _qkv_rm_kernel,
        grid=(B, S // ts),
        in_specs=[seq(D), pl.BlockSpec((D, 3 * W), lambda b, t: (0, 0), pipeline_mode=pl.Buffered(1)),
                  seq(HEAD_DIM), seq(HEAD_DIM)],
        out_specs=[out_spec] * 3,
        out_shape=[jax.ShapeDtypeStruct((B, R, I, W), F32)] * 3,
        scratch_shapes=[pltpu.VMEM((D // HEAD_DIM, ts, HEAD_DIM), F32)],
        compiler_params=_params("arbitrary", "arbitrary"),
        name="qkv_rm",
    )(x3, w_bf16, cos, sin)


def _pattern_bias(dil):
    nslab = ATT_RES // dil
    slab = ATT_BLOCK // nslab
    row = np.arange(ATT_BLOCK)
    col = np.arange(2 * ATT_BLOCK)
    jq = nslab * (row % slab) + row // slab + ATT_BLOCK
    jk = nslab * (col % slab) + (col % ATT_BLOCK) // slab + ATT_BLOCK * (col // ATT_BLOCK)
    delta = jq[:, None] - jk[None, :]
    return np.where((delta >= 0) & (delta <= ATT_BLOCK), 0.0, -np.inf).astype(np.float32)


def _att_rm_kernel(q_ref, k_ref, v_ref, bias_ref, o_ref, num_ref, den_ref, mx_ref, onat_ref):
    QB = ATT_BLOCK
    TI = ATT_TILE // ATT_RES
    ones = jnp.ones((2 * QB, HEAD_DIM), BF16)

    def gather(ref, slabs, r0, rn):
        return jnp.concatenate([ref[r, r0:r0 + rn, :] for r in slabs], axis=0)

    for tile in range(q_ref.shape[1] // TI):
        i0 = tile * TI
        for p, (window, dil) in enumerate(DIL_PATTERNS):
            assert window // dil == QB
            nslab = ATT_RES // dil
            slab = QB // nslab
            for n in range(ATT_TILE // QB):
                res, m = n % dil, n // dil
                slabs = [res + dil * a for a in range(nslab)]
                c0 = i0 + slab * m
                qb = gather(q_ref, slabs, c0, slab).astype(BF16)
                if c0 == 0:
                    kb = gather(k_ref, slabs, c0, slab).astype(BF16)
                    vb = gather(v_ref, slabs, c0, slab).astype(BF16)
                    bias = bias_ref[p, :, QB:2 * QB]
                else:
                    kb = jnp.concatenate([gather(k_ref, slabs, c0 - slab, slab),
                                          gather(k_ref, slabs, c0, slab)], axis=0).astype(BF16)
                    vb = jnp.concatenate([gather(v_ref, slabs, c0 - slab, slab),
                                          gather(v_ref, slabs, c0, slab)], axis=0).astype(BF16)
                    bias = bias_ref[p]
                s = _dot_nt(qb, kb) + bias
                mx = jnp.max(s, axis=-1, keepdims=True)
                e = jnp.exp2(s - mx).astype(BF16)
                nd = _dot(e, jnp.concatenate([vb, ones[0:vb.shape[0]]], axis=1))
                mxb = jnp.broadcast_to(mx, (QB, HEAD_DIM))
                for a, r in enumerate(slabs):
                    rows = slice(a * slab, (a + 1) * slab)
                    dst = slice(slab * m, slab * (m + 1))
                    num_ref[p, r, dst, :] = nd[rows, 0:HEAD_DIM]
                    den_ref[p, r, dst, :] = nd[rows, HEAD_DIM:2 * HEAD_DIM]
                    mx_ref[p, r, dst, :] = mxb[rows]

        m0, m1, m2 = mx_ref[0], mx_ref[1], mx_ref[2]
        mm = jnp.maximum(jnp.maximum(m0, m1), m2)
        w0, w1, w2 = jnp.exp2(m0 - mm), jnp.exp2(m1 - mm), jnp.exp2(m2 - mm)
        num = w0 * num_ref[0] + w1 * num_ref[1] + w2 * num_ref[2]
        den = w0 * den_ref[0] + w1 * den_ref[1] + w2 * den_ref[2]
        out = num / den
        for r in range(ATT_RES):
            onat_ref[pl.ds(r, TI, stride=ATT_RES), :] = out[r]
        o_ref[tile * ATT_TILE:(tile + 1) * ATT_TILE, :] = onat_ref[...].astype(o_ref.dtype)


def _att_rm(q, k, v):
    B, R, I, W = q.shape
    TI = ATT_TILE // ATT_RES
    assert R == ATT_RES and I % TI == 0
    bias = jnp.asarray(np.stack([_pattern_bias(dil) for _, dil in DIL_PATTERNS]))
    blk = pl.BlockSpec((None, R, I, HEAD_DIM), lambda b, h: (b, 0, 0, h))
    acc = pltpu.VMEM((len(DIL_PATTERNS), R, TI, HEAD_DIM), F32)
    return pl.pallas_call(
        _att_rm_kernel,
        grid=(B, N_HEADS),
        in_specs=[blk, blk, blk,
                  pl.BlockSpec(bias.shape, lambda b, h: (0, 0, 0), pipeline_mode=pl.Buffered(1))],
        out_specs=pl.BlockSpec((None, R * I, HEAD_DIM), lambda b, h: (b, 0, h)),
        out_shape=jax.ShapeDtypeStruct((B, R * I, W), BF16),
        scratch_shapes=[acc, acc, acc, pltpu.VMEM((ATT_TILE, HEAD_DIM), F32)],
        compiler_params=_params("arbitrary", "arbitrary"),
        name="att_rm",
    )(q, k, v, bias)


FFN_SUB = 512
FFN_CHUNK = 256


def _mix_ffn_kernel(x_ref, a1_ref, a2_ref, wm_ref, wi_ref, wo_ref, g_ref, b_ref, o_ref, *, tm):
    W = HALF_WIDTH
    nsub = tm // FFN_SUB
    h1 = [None] * nsub
    hb = [None] * nsub

    def prologue(i):
        rows = slice(i * FFN_SUB, (i + 1) * FFN_SUB)
        mix = _dot(a1_ref[rows, :], wm_ref[0:W, :]) + _dot(a2_ref[rows, :], wm_ref[W:2 * W, :])
        h1[i] = _layer_norm(DEEPNORM_ALPHA * x_ref[rows, :] + mix, g_ref[0:1, :], b_ref[0:1, :])
        hb[i] = h1[i].astype(BF16)

    def epilogue(i, acc):
        rows = slice(i * FFN_SUB, (i + 1) * FFN_SUB)
        o_ref[rows, :] = _layer_norm(DEEPNORM_ALPHA * h1[i] + acc, g_ref[1:2, :], b_ref[1:2, :])

    prologue(0)
    prev_acc = None
    for i in range(nsub):
        acc = None
        for ci, c0 in enumerate(range(0, D_FF, FFN_CHUNK)):
            if ci == 1 and i + 1 < nsub:
                prologue(i + 1)
            if ci == 1 and i > 0:
                epilogue(i - 1, prev_acc)
            gate = _dot(hb[i], wi_ref[:, c0:c0 + FFN_CHUNK])
            up = _dot(hb[i], wi_ref[:, D_FF + c0:D_FF + c0 + FFN_CHUNK])
            part = _dot((_silu(gate) * up).astype(BF16), wo_ref[c0:c0 + FFN_CHUNK, :])
            acc = part if acc is None else acc + part
        prev_acc = acc
    epilogue(nsub - 1, prev_acc)


def _mix_ffn(x2d, a1, a2, w_mix_bf16, w_in_bf16, w_out_bf16, g2, b2):
    T, D = x2d.shape
    tm = 1024
    rowblk = lambda w: pl.BlockSpec((tm, w), lambda i: (i, 0))
    resident = lambda shape: pl.BlockSpec(shape, lambda i: (0, 0), pipeline_mode=pl.Buffered(1))
    return pl.pallas_call(
        functools.partial(_mix_ffn_kernel, tm=tm),
        grid=(T // tm,),
        in_specs=[rowblk(D), rowblk(HALF_WIDTH), rowblk(HALF_WIDTH), resident((D, D)),
                  resident((D, 2 * D_FF)), resident((D_FF, D)), resident((2, D)), resident((2, D))],
        out_specs=rowblk(D),
        out_shape=jax.ShapeDtypeStruct((T, D), F32),
        compiler_params=_params("arbitrary"),
        name="mix_ffn",
    )(x2d, a1, a2, w_mix_bf16, w_in_bf16, w_out_bf16, g2, b2)


def kernel(x, positions, ev_w_in, ev_ret_norm_g, ev_pool_w, ev_pool_scale, ev_w_out, od_w_in, od_conv_w, od_conv_b, od_gate_a_w, od_gate_a_b, od_gate_x_w, od_gate_x_b, od_lru_lambda, od_w_out, ffn_w_in, ffn_w_out, ln_g, ln_b):
    B, S, D = x.shape
    T = B * S
    cos, sin = _rope_tables(positions)
    h = x.reshape(T, D)
    for layer in range(DEPTH):
        j = layer // 2
        h3 = h.reshape(B, S, D)
        if layer % 2 == 0:
            a1, a2 = _even_core(h3, ev_w_in[j].astype(BF16), cos, sin, ev_ret_norm_g[j],
                                ev_pool_w[j].astype(BF16), ev_pool_scale[j])
            w_mix = ev_w_out[j]
        else:
            w_in = od_w_in[j].astype(BF16)
            a1 = _lru_core(h3, w_in[:, :2 * HALF_WIDTH], od_conv_w[j], od_conv_b[j], od_gate_a_w[j], od_gate_a_b[j],
                           od_gate_x_w[j], od_gate_x_b[j], od_lru_lambda[j])
            a2 = _att_rm(*_qkv_rm(h3, w_in[:, 2 * HALF_WIDTH:], cos, sin))
            w_mix = od_w_out[j]
        h = _mix_ffn(h, a1.reshape(T, HALF_WIDTH), a2.reshape(T, HALF_WIDTH), w_mix.astype(BF16),
                     ffn_w_in[layer].astype(BF16), ffn_w_out[layer].astype(BF16), ln_g[layer], ln_b[layer])
    return h.reshape(B, S, D)
```

```python
import functools
import math

import jax
import jax.numpy as jnp
import numpy as np
from jax import lax
from jax.experimental import pallas as pl
from jax.experimental.pallas import tpu as pltpu

F32 = jnp.float32
BF16 = jnp.bfloat16

D_MODEL = 1024
DEPTH = 4
HEAD_DIM = 128
HALF_WIDTH = 512
N_HEADS = HALF_WIDTH // HEAD_DIM
RET_CHUNK = 128
POOL_WINDOWS = (2, 4, 8, 16)
POOL_GROUP = 128
POOL_HALO = 32
LRU_BLOCKS = 8
LRU_C = 8.0
CONV_WIDTH = 4
CONV_HALO = 8
DIL_PATTERNS = ((128, 1), (512, 4), (2048, 16))
ATT_BLOCK = 128
ATT_TILE = 2048
ATT_RES = 16
ROPE_THETA = 10000.0
EVEN_IN = 5 * HALF_WIDTH
ODD_IN = 5 * HALF_WIDTH
D_FF = 2816
DEEPNORM_ALPHA = (2 * DEPTH) ** 0.25
LN_EPS = 1e-5

VMEM_LIMIT = 56 * 1024 * 1024


def _params(*sem):
    return pltpu.CompilerParams(dimension_semantics=sem, vmem_limit_bytes=VMEM_LIMIT)


def _sigmoid(x):
    return 1.0 / (1.0 + jnp.exp(-x))


def _silu(x):
    return x * _sigmoid(x)


def _gelu_tanh(x):
    c = math.sqrt(2.0 / math.pi)
    return x * (0.5 * (1.0 + jnp.tanh(c * (x + 0.044715 * (x * x * x)))))


def _layer_norm(y, g, b):
    mu = jnp.mean(y, axis=-1, keepdims=True)
    d = y - mu
    var = jnp.mean(d * d, axis=-1, keepdims=True)
    return d * lax.rsqrt(var + LN_EPS) * g + b


def _rope(t, cos, sin_signed):
    return t * cos + pltpu.roll(t, HEAD_DIM // 2, axis=1) * sin_signed


def _dot(a, b):
    return jnp.dot(a, b, preferred_element_type=F32)


def _dot_nt(a, b):
    return lax.dot_general(a, b, (((1,), (1,)), ((), ())), preferred_element_type=F32)


def _rope_table_kernel(pos_ref, inv_ref, cos_ref, sin_ref):
    ts = pos_ref.shape[0]
    hs = ts // 2
    half = HEAD_DIM // 2
    lo = lax.broadcasted_iota(jnp.int32, (1, HEAD_DIM), 1) < half
    pos = pos_ref[...].astype(F32)
    ang = jnp.where(lo, pos[0:hs], pos[hs:ts]) * inv_ref[...]
    c = jnp.cos(ang)
    s = jnp.sin(ang)
    cr = pltpu.roll(c, half, axis=1)
    sr = pltpu.roll(s, half, axis=1)
    cos_ref[0:hs, :] = jnp.where(lo, c, cr)
    cos_ref[hs:ts, :] = jnp.where(lo, cr, c)
    sin_ref[0:hs, :] = jnp.where(lo, -s, sr)
    sin_ref[hs:ts, :] = jnp.where(lo, -sr, s)


def _rope_tables(positions):
    B, S = positions.shape
    T = B * S
    ts = 2048
    inv = ROPE_THETA ** (-jnp.arange(0, HEAD_DIM, 2, dtype=F32) / HEAD_DIM)
    inv2 = jnp.concatenate([inv, inv])[None, :]
    row = pl.BlockSpec((1, HEAD_DIM), lambda i: (0, 0))
    cos, sin = pl.pallas_call(
        _rope_table_kernel,
        grid=(T // ts,),
        in_specs=[pl.BlockSpec((ts, 1), lambda i: (i, 0)), row],
        out_specs=[pl.BlockSpec((ts, HEAD_DIM), lambda i: (i, 0))] * 2,
        out_shape=[jax.ShapeDtypeStruct((T, HEAD_DIM), F32)] * 2,
        compiler_params=_params("arbitrary"),
        name="rope_tables",
    )(positions.reshape(T, 1), inv2)
    return cos.reshape(B, S, HEAD_DIM), sin.reshape(B, S, HEAD_DIM)


def _seq_specs(B, nblk, ts):
    last = B * nblk - 1
    cur = lambda w: pl.BlockSpec((None, ts, w), lambda s: (s // nblk, s % nblk, 0))
    first = lambda w: pl.BlockSpec((None, ts, w), lambda s: (0, 0, 0))

    def ahead(w):
        def index_map(s):
            n = jnp.minimum(s + 1, last)
            return (n // nblk, n % nblk, 0)
        return pl.BlockSpec((None, ts, w), index_map)

    return cur, ahead, first


def _project_ahead(step, x0_ref, xn_ref, w_ref, xb_ref, za_ref, zb_ref, body):
    @pl.when(step == 0)
    def _():
        za_ref[...] = _dot(x0_ref[...].astype(BF16), w_ref[...])

    def run(z_cur_ref, z_next_ref):
        def project_piece(k, n):
            if k == 0:
                xb_ref[...] = xn_ref[...].astype(BF16)
            width = w_ref.shape[1] // n
            cols = slice(k * width, (k + 1) * width)
            z_next_ref[:, cols] = _dot(xb_ref[...], w_ref[:, cols])
        body(z_cur_ref, project_piece)

    @pl.when(step % 2 == 0)
    def _():
        run(za_ref, zb_ref)

    @pl.when(step % 2 == 1)
    def _():
        run(zb_ref, za_ref)


def _even_kernel(x0_ref, xn_ref, wi_ref, cos_ref, sin_ref, decay_ref, qdec_ref, kdec_ref, cdec_ref, gn_ref, pw_ref,
                 ps_ref, ret_ref, pool_ref, xb_ref, za_ref, zb_ref, state_ref, pbuf_ref, s2_ref, s4_ref, s8_ref,
                 *, ts, nblk):
    step = pl.program_id(0)
    sblk = step % nblk
    H = POOL_HALO

    @pl.when(sblk == 0)
    def _():
        state_ref[...] = jnp.zeros_like(state_ref)
        pbuf_ref[0:H, :] = jnp.zeros((H, HALF_WIDTH), F32)

    body = functools.partial(_even_body, cos_ref, sin_ref, decay_ref, qdec_ref, kdec_ref, cdec_ref, gn_ref, pw_ref,
                             ps_ref, ret_ref, pool_ref, state_ref, pbuf_ref, s2_ref, s4_ref, s8_ref, sblk, ts)
    _project_ahead(step, x0_ref, xn_ref, wi_ref, xb_ref, za_ref, zb_ref, body)


def _block_diag2(a, b):
    za = jnp.zeros_like(a)
    zb = jnp.zeros_like(b)
    return jnp.concatenate([jnp.concatenate([a, zb], axis=1), jnp.concatenate([za, b], axis=1)], axis=0)


def _pair(ref, hs):
    return jnp.concatenate([ref[h] for h in hs], axis=1)


def _even_body(cos_ref, sin_ref, decay_ref, qdec_ref, kdec_ref, cdec_ref, gn_ref, pw_ref, ps_ref, ret_ref, pool_ref,
               state_ref, pbuf_ref, s2_ref, s4_ref, s8_ref, sblk, ts, z_ref, project_piece):
    H = POOL_HALO
    npieces = ts // RET_CHUNK + 1
    for c in range(ts // RET_CHUNK):
        project_piece(c, npieces)
        rows = pl.ds(c * RET_CHUNK, RET_CHUNK)
        cos = cos_ref[rows, :]
        sin = sin_ref[rows, :]
        for h0 in range(0, N_HEADS, 2):
            hs = (h0, h0 + 1)
            q = [_rope(z_ref[rows, pl.ds(h * HEAD_DIM, HEAD_DIM)], cos, sin) for h in hs]
            k = [_rope(z_ref[rows, pl.ds(HALF_WIDTH + h * HEAD_DIM, HEAD_DIM)], cos, sin) * (HEAD_DIM ** -0.5)
                 for h in hs]
            vb = [z_ref[rows, pl.ds(2 * HALF_WIDTH + h * HEAD_DIM, HEAD_DIM)].astype(BF16) for h in hs]
            state = [state_ref[h] for h in hs]
            qb = jnp.concatenate([t.astype(BF16) for t in q], axis=1)
            scores = _dot_nt(qb, _block_diag2(*[t.astype(BF16) for t in k])) * _pair(decay_ref, hs)
            vbd = _block_diag2(*vb)
            inner = _dot(scores.astype(BF16), vbd)
            cross = _dot(qb, _block_diag2(*[s.astype(BF16) for s in state])) * _pair(qdec_ref, hs)
            kd = jnp.concatenate([(k[i] * kdec_ref[h]).T.astype(BF16) for i, h in enumerate(hs)], axis=1)
            kv = _dot(kd, vbd)
            y2 = inner + cross
            for i, h in enumerate(hs):
                lanes = slice(h * HEAD_DIM, (h + 1) * HEAD_DIM)
                half = slice(i * HEAD_DIM, (i + 1) * HEAD_DIM)
                state_ref[h] = state[i] * cdec_ref[h] + kv[:, half]
                y = y2[:, half]
                gate = z_ref[rows, pl.ds(3 * HALF_WIDTH + h * HEAD_DIM, HEAD_DIM)]
                mu = jnp.mean(y, axis=-1, keepdims=True)
                d = y - mu
                var = jnp.mean(d * d, axis=-1, keepdims=True)
                yn = d * lax.rsqrt(var + LN_EPS) * gn_ref[:, lanes]
                ret_ref[rows, lanes] = (yn * _silu(gate)).astype(ret_ref.dtype)

    project_piece(npieces - 1, npieces)
    pbuf_ref[H:H + ts, :] = z_ref[:, pl.ds(4 * HALF_WIDTH, HALF_WIDTH)]
    G = POOL_GROUP
    s2_ref[8:H + ts, :] = pbuf_ref[8:H + ts, :] + pbuf_ref[7:H + ts - 1, :]
    s4_ref[16:H + ts, :] = s2_ref[16:H + ts, G:4 * G] + s2_ref[14:H + ts - 2, G:4 * G]
    s8_ref[24:H + ts, :] = s4_ref[24:H + ts, G:3 * G] + s4_ref[20:H + ts - 4, G:3 * G]
    s16 = s8_ref[32:H + ts, G:2 * G] + s8_ref[24:H + ts - 8, G:2 * G]
    wsums = (s2_ref[H:H + ts, 0:G], s4_ref[H:H + ts, 0:G], s8_ref[H:H + ts, 0:G], s16)
    tpos = sblk * ts + lax.broadcasted_iota(jnp.int32, (ts, 1), 0) + 1
    for gi, w in enumerate(POOL_WINDOWS):
        lanes = slice(gi * G, (gi + 1) * G)
        cnt = jnp.minimum(tpos, w).astype(F32)
        pooled = wsums[gi] / cnt - pbuf_ref[H:H + ts, lanes]
        mixed = _dot(pooled.astype(BF16), pw_ref[gi]) * ps_ref[:, lanes]
        pool_ref[:, lanes] = mixed.astype(pool_ref.dtype)
    pbuf_ref[0:H, :] = pbuf_ref[ts:ts + H, :]


def _even_core(x3, w_in_bf16, cos, sin, ret_norm_g, pool_w_bf16, pool_scale):
    B, S, D = x3.shape
    ts = 512
    C = RET_CHUNK
    lg = jnp.log1p(-(2.0 ** (-5.0 - jnp.arange(N_HEADS, dtype=F32))))
    idx = jnp.arange(C, dtype=F32)
    rel = idx[:, None] - idx[None, :]
    decay = jnp.where(rel[None] >= 0, jnp.exp(jnp.maximum(rel, 0.0)[None] * lg[:, None, None]), 0.0)
    k_decay = jnp.exp((C - 1 - idx)[None, :] * lg[:, None])
    q_decay = jnp.exp((idx + 1.0)[None, :] * lg[:, None])
    chunk_decay = jnp.exp(C * lg)
    qdec = jnp.broadcast_to(q_decay[:, :, None], (N_HEADS, C, HEAD_DIM))
    kdec = jnp.broadcast_to(k_decay[:, :, None], (N_HEADS, C, HEAD_DIM))
    cdec = jnp.broadcast_to(chunk_decay[:, None, None], (N_HEADS, 1, HEAD_DIM))

    const3 = lambda shape: pl.BlockSpec(shape, lambda s: (0, 0, 0), pipeline_mode=pl.Buffered(1))
    const2 = lambda shape: pl.BlockSpec(shape, lambda s: (0, 0), pipeline_mode=pl.Buffered(1))
    nblk = S // ts
    seq, ahead, first = _seq_specs(B, nblk, ts)
    return pl.pallas_call(
        functools.partial(_even_kernel, ts=ts, nblk=nblk),
        grid=(B * nblk,),
        in_specs=[first(D), ahead(D), const2((D, EVEN_IN)), seq(HEAD_DIM), seq(HEAD_DIM),
                  const3((N_HEADS, C, C)), const3((N_HEADS, C, HEAD_DIM)), const3((N_HEADS, C, HEAD_DIM)),
                  const3((N_HEADS, 1, HEAD_DIM)), const2((1, HALF_WIDTH)),
                  const3((len(POOL_WINDOWS), POOL_GROUP, POOL_GROUP)), const2((1, HALF_WIDTH))],
        out_specs=[seq(HALF_WIDTH), seq(HALF_WIDTH)],
        out_shape=[jax.ShapeDtypeStruct((B, S, HALF_WIDTH), BF16)] * 2,
        scratch_shapes=[pltpu.VMEM((ts, D), BF16), pltpu.VMEM((ts, EVEN_IN), F32), pltpu.VMEM((ts, EVEN_IN), F32),
                        pltpu.VMEM((N_HEADS, HEAD_DIM, HEAD_DIM), F32),
                        pltpu.VMEM((POOL_HALO + ts, HALF_WIDTH), F32),
                        pltpu.VMEM((POOL_HALO + ts, HALF_WIDTH), F32),
                        pltpu.VMEM((POOL_HALO + ts, 3 * POOL_GROUP), F32),
                        pltpu.VMEM((POOL_HALO + ts, 2 * POOL_GROUP), F32)],
        compiler_params=_params("arbitrary"),
        name="even_core",
    )(x3, x3, w_in_bf16, cos, sin, decay, qdec, kdec, cdec, ret_norm_g[None, :], pool_w_bf16, pool_scale[None, :])


def _lru_kernel(x_ref, wi_ref, cw_ref, cb_ref, wa_ref, ba_ref, wx_ref, bx_ref, lam_ref, y_ref,
                zg_ref, ubuf_ref, hcar_ref, h_ref, *, ts, nblk):
    step = pl.program_id(0)
    H = CONV_HALO

    @pl.when(step % nblk == 0)
    def _():
        ubuf_ref[0:H, :] = jnp.zeros((H, HALF_WIDTH), F32)
        hcar_ref[...] = jnp.zeros_like(hcar_ref)

    xb = x_ref[...].astype(BF16)
    ubuf_ref[H:H + ts, :] = _dot(xb, wi_ref[:, HALF_WIDTH:2 * HALF_WIDTH])
    zg_ref[...] = _dot(xb, wi_ref[:, 0:HALF_WIDTH])
    u = cb_ref[...] + cw_ref[CONV_WIDTH - 1:CONV_WIDTH, :] * ubuf_ref[H:H + ts, :]
    for k in range(CONV_WIDTH - 1):
        off = H - (CONV_WIDTH - 1) + k
        u = u + cw_ref[k:k + 1, :] * ubuf_ref[off:off + ts, :]
    ubuf_ref[0:H, :] = ubuf_ref[ts:ts + H, :]

    ub = u.astype(BF16)
    r = _sigmoid(_dot(ub, wa_ref[...]) + ba_ref[...])
    ig = _sigmoid(_dot(ub, wx_ref[...]) + bx_ref[...])
    nl = -lam_ref[...]
    softplus = jnp.maximum(nl, 0.0) + jnp.log1p(jnp.exp(-jnp.abs(nl)))
    log_a = (-LRU_C * r) * softplus
    a = jnp.exp(log_a)
    bseq = jnp.sqrt(jnp.tanh(-log_a) * (1.0 + a * a)) * (ig * u)

    G = 8
    a = a.reshape(ts // G, G, HALF_WIDTH)
    bseq = bseq.reshape(ts // G, G, HALF_WIDTH)
    row = lax.broadcasted_iota(jnp.int32, (1, G, 1), 1)
    sh = 1
    while sh < G:
        a_prev = pltpu.roll(a, sh, axis=1)
        b_prev = pltpu.roll(bseq, sh, axis=1)
        m = row >= sh
        bseq = jnp.where(m, a * b_prev + bseq, bseq)
        a = jnp.where(m, a * a_prev, a)
        sh *= 2
    carry = hcar_ref[0:1, :]
    for v in range(ts // G):
        hv = bseq[v] + a[v] * carry
        h_ref[G * v:G * (v + 1), :] = hv
        carry = hv[G - 1:G]
    hcar_ref[...] = jnp.broadcast_to(carry, hcar_ref.shape)
    y_ref[...] = (h_ref[...] * _gelu_tanh(zg_ref[...])).astype(y_ref.dtype)


def _block_diag(w):
    n, c, d = w.shape
    eye = jnp.eye(n, dtype=w.dtype)
    return (w[:, :, None, :] * eye[:, None, :, None]).reshape(n * c, n * d)


def _lru_core(x3, w_in_bf16, conv_w, conv_b, gate_a_w, gate_a_b, gate_x_w, gate_x_b, lam):
    B, S, D = x3.shape
    ts = 512
    W = HALF_WIDTH
    wa = _block_diag(gate_a_w).astype(BF16)
    wx = _block_diag(gate_x_w).astype(BF16)
    const = lambda shape: pl.BlockSpec(shape, lambda s: (0, 0), pipeline_mode=pl.Buffered(1))
    nblk = S // ts
    seq, _, _ = _seq_specs(B, nblk, ts)
    return pl.pallas_call(
        functools.partial(_lru_kernel, ts=ts, nblk=nblk),
        grid=(B * nblk,),
        in_specs=[seq(D), const((D, 2 * W)),
                  const((CONV_WIDTH, W)), const((1, W)), const((W, W)), const((1, W)),
                  const((W, W)), const((1, W)), const((1, W))],
        out_specs=seq(W),
        out_shape=jax.ShapeDtypeStruct((B, S, W), BF16),
        scratch_shapes=[pltpu.VMEM((ts, W), F32),
                        pltpu.VMEM((CONV_HALO + ts, W), F32), pltpu.VMEM((8, W), F32), pltpu.VMEM((ts, W), F32)],
        compiler_params=_params("arbitrary"),
        name="lru_core",
    )(x3, w_in_bf16, conv_w, conv_b[None, :], wa, gate_a_b[None, :], wx, gate_x_b[None, :], lam[None, :])


def _qkv_rm_kernel(x_ref, w_ref, cos_ref, sin_ref, q_ref, k_ref, v_ref, xs_ref):
    R = ATT_RES
    rows_per = x_ref.shape[0] // R
    ncol = x_ref.shape[1] // HEAD_DIM

    def residue_rows(ref, r):
        return ref[pl.ds(r, rows_per, stride=R), :]

    for c in range(ncol):
        xs_ref[c] = x_ref[:, c * HEAD_DIM:(c + 1) * HEAD_DIM]
    qscale = HEAD_DIM ** -0.5 * math.log2(math.e)
    half = R // 2
    for r0 in (0, half):
        x = jnp.concatenate(
            [jnp.concatenate([residue_rows(xs_ref.at[c], r) for c in range(ncol)], axis=1)
             for r in range(r0, r0 + half)], axis=0).astype(BF16)
        z = _dot(x, w_ref[...])
        for r in range(r0, r0 + half):
            rows = slice((r - r0) * rows_per, (r - r0 + 1) * rows_per)
            cos = residue_rows(cos_ref, r)
            sin = residue_rows(sin_ref, r)
            for h in range(N_HEADS):
                lanes = slice(h * HEAD_DIM, (h + 1) * HEAD_DIM)
                q_ref[r, :, lanes] = _rope(z[rows, h * HEAD_DIM:(h + 1) * HEAD_DIM], cos, sin) * qscale
                k_ref[r, :, lanes] = _rope(z[rows, HALF_WIDTH + h * HEAD_DIM:HALF_WIDTH + (h + 1) * HEAD_DIM],
                                           cos, sin)
            v_ref[r] = z[rows, 2 * HALF_WIDTH:3 * HALF_WIDTH]


def _qkv_rm(x3, w_bf16, cos, sin):
    B, S, D = x3.shape
    R = ATT_RES
    I = S // R
    W = HALF_WIDTH
    ts = 1024
    seq = lambda w: pl.BlockSpec((None, ts, w), lambda b, t: (b, t, 0))
    out_spec = pl.BlockSpec((None, R, ts // R, W), lambda b, t: (b, 0, t, 0))
    return pl.pallas_call(
        _qkv_rm_kernel,
        grid=(B, S // ts),
        in_specs=[seq(D), pl.BlockSpec((D, 3 * W), lambda b, t: (0, 0), pipeline_mode=pl.Buffered(1)),
                  seq(HEAD_DIM), seq(HEAD_DIM)],
        out_specs=[out_spec] * 3,
        out_shape=[jax.ShapeDtypeStruct((B, R, I, W), F32)] * 3,
        scratch_shapes=[pltpu.VMEM((D // HEAD_DIM, ts, HEAD_DIM), F32)],
        compiler_params=_params("arbitrary", "arbitrary"),
        name="qkv_rm",
    )(x3, w_bf16, cos, sin)


def _pattern_bias(dil):
    nslab = ATT_RES // dil
    slab = ATT_BLOCK // nslab
    row = np.arange(ATT_BLOCK)
    col = np.arange(2 * ATT_BLOCK)
    jq = nslab * (row % slab) + row // slab + ATT_BLOCK
    jk = nslab * (col % slab) + (col % ATT_BLOCK) // slab + ATT_BLOCK * (col // ATT_BLOCK)
    delta = jq[:, None] - jk[None, :]
    return np.where((delta >= 0) & (delta <= ATT_BLOCK), 0.0, -np.inf).astype(np.float32)


def _att_rm_kernel(q_ref, k_ref, v_ref, bias_ref, o_ref, num_ref, den_ref, mx_ref, onat_ref):
    QB = ATT_BLOCK
    TI = ATT_TILE // ATT_RES
    ones = jnp.ones((2 * QB, HEAD_DIM), BF16)

    def gather(ref, slabs, r0, rn):
        return jnp.concatenate([ref[r, r0:r0 + rn, :] for r in slabs], axis=0)

    for tile in range(q_ref.shape[1] // TI):
        i0 = tile * TI
        for p, (window, dil) in enumerate(DIL_PATTERNS):
            assert window // dil == QB
            nslab = ATT_RES // dil
            slab = QB // nslab
            for n in range(ATT_TILE // QB):
                res, m = n % dil, n // dil
                slabs = [res + dil * a for a in range(nslab)]
                c0 = i0 + slab * m
                qb = gather(q_ref, slabs, c0, slab).astype(BF16)
                if c0 == 0:
                    kb = gather(k_ref, slabs, c0, slab).astype(BF16)
                    vb = gather(v_ref, slabs, c0, slab).astype(BF16)
                    bias = bias_ref[p, :, QB:2 * QB]
                else:
                    kb = jnp.concatenate([gather(k_ref, slabs, c0 - slab, slab),
                                          gather(k_ref, slabs, c0, slab)], axis=0).astype(BF16)
                    vb = jnp.concatenate([gather(v_ref, slabs, c0 - slab, slab),
                                          gather(v_ref, slabs, c0, slab)], axis=0).astype(BF16)
                    bias = bias_ref[p]
                s = _dot_nt(qb, kb) + bias
                mx = jnp.max(s, axis=-1, keepdims=True)
                e = jnp.exp2(s - mx).astype(BF16)
                nd = _dot(e, jnp.concatenate([vb, ones[0:vb.shape[0]]], axis=1))
                mxb = jnp.broadcast_to(mx, (QB, HEAD_DIM))
                for a, r in enumerate(slabs):
                    rows = slice(a * slab, (a + 1) * slab)
                    dst = slice(slab * m, slab * (m + 1))
                    num_ref[p, r, dst, :] = nd[rows, 0:HEAD_DIM]
                    den_ref[p, r, dst, :] = nd[rows, HEAD_DIM:2 * HEAD_DIM]
                    mx_ref[p, r, dst, :] = mxb[rows]

        m0, m1, m2 = mx_ref[0], mx_ref[1], mx_ref[2]
        mm = jnp.maximum(jnp.maximum(m0, m1), m2)
        w0, w1, w2 = jnp.exp2(m0 - mm), jnp.exp2(m1 - mm), jnp.exp2(m2 - mm)
        num = w0 * num_ref[0] + w1 * num_ref[1] + w2 * num_ref[2]
        den = w0 * den_ref[0] + w1 * den_ref[1] + w2 * den_ref[2]
        out = num / den
        for r in range(ATT_RES):
            onat_ref[pl.ds(r, TI, stride=ATT_RES), :] = out[r]
        o_ref[tile * ATT_TILE:(tile + 1) * ATT_TILE, :] = onat_ref[...].astype(o_ref.dtype)


def _att_rm(q, k, v):
    B, R, I, W = q.shape
    TI = ATT_TILE // ATT_RES
    assert R == ATT_RES and I % TI == 0
    bias = jnp.asarray(np.stack([_pattern_bias(dil) for _, dil in DIL_PATTERNS]))
    blk = pl.BlockSpec((None, R, I, HEAD_DIM), lambda b, h: (b, 0, 0, h))
    acc = pltpu.VMEM((len(DIL_PATTERNS), R, TI, HEAD_DIM), F32)
    return pl.pallas_call(
        _att_rm_kernel,
        grid=(B, N_HEADS),
        in_specs=[blk, blk, blk,
                  pl.BlockSpec(bias.shape, lambda b, h: (0, 0, 0), pipeline_mode=pl.Buffered(1))],
        out_specs=pl.BlockSpec((None, R * I, HEAD_DIM), lambda b, h: (b, 0, h)),
        out_shape=jax.ShapeDtypeStruct((B, R * I, W), BF16),
        scratch_shapes=[acc, acc, acc, pltpu.VMEM((ATT_TILE, HEAD_DIM), F32)],
        compiler_params=_params("arbitrary", "arbitrary"),
        name="att_rm",
    )(q, k, v, bias)


FFN_SUB = 512
FFN_CHUNK = 256


def _mix_ffn_kernel(x_ref, a1_ref, a2_ref, wm_ref, wi_ref, wo_ref, g_ref, b_ref, o_ref, *, tm):
    W = HALF_WIDTH
    nsub = tm // FFN_SUB
    h1 = [None] * nsub
    hb = [None] * nsub

    def prologue(i):
        rows = slice(i * FFN_SUB, (i + 1) * FFN_SUB)
        mix = _dot(a1_ref[rows, :], wm_ref[0:W, :]) + _dot(a2_ref[rows, :], wm_ref[W:2 * W, :])
        h1[i] = _layer_norm(DEEPNORM_ALPHA * x_ref[rows, :] + mix, g_ref[0:1, :], b_ref[0:1, :])
        hb[i] = h1[i].astype(BF16)

    def epilogue(i, acc):
        rows = slice(i * FFN_SUB, (i + 1) * FFN_SUB)
        o_ref[rows, :] = _layer_norm(DEEPNORM_ALPHA * h1[i] + acc, g_ref[1:2, :], b_ref[1:2, :])

    prologue(0)
    prev_acc = None
    for i in range(nsub):
        acc = None
        for ci, c0 in enumerate(range(0, D_FF, FFN_CHUNK)):
            if ci == 1 and i + 1 < nsub:
                prologue(i + 1)
            if ci == 1 and i > 0:
                epilogue(i - 1, prev_acc)
            gate = _dot(hb[i], wi_ref[:, c0:c0 + FFN_CHUNK])
            up = _dot(hb[i], wi_ref[:, D_FF + c0:D_FF + c0 + FFN_CHUNK])
            part = _dot((_silu(gate) * up).astype(BF16), wo_ref[c0:c0 + FFN_CHUNK, :])
            acc = part if acc is None else acc + part
        prev_acc = acc
    epilogue(nsub - 1, prev_acc)


def _mix_ffn(x2d, a1, a2, w_mix_bf16, w_in_bf16, w_out_bf16, g2, b2):
    T, D = x2d.shape
    tm = 1024
    rowblk = lambda w: pl.BlockSpec((tm, w), lambda i: (i, 0))
    resident = lambda shape: pl.BlockSpec(shape, lambda i: (0, 0), pipeline_mode=pl.Buffered(1))
    return pl.pallas_call(
        functools.partial(_mix_ffn_kernel, tm=tm),
        grid=(T // tm,),
        in_specs=[rowblk(D), rowblk(HALF_WIDTH), rowblk(HALF_WIDTH), resident((D, D)),
                  resident((D, 2 * D_FF)), resident((D_FF, D)), resident((2, D)), resident((2, D))],
        out_specs=rowblk(D),
        out_shape=jax.ShapeDtypeStruct((T, D), F32),
        compiler_params=_params("arbitrary"),
        name="mix_ffn",
    )(x2d, a1, a2, w_mix_bf16, w_in_bf16, w_out_bf16, g2, b2)


def kernel(x, positions, ev_w_in, ev_ret_norm_g, ev_pool_w, ev_pool_scale, ev_w_out, od_w_in, od_conv_w, od_conv_b, od_gate_a_w, od_gate_a_b, od_gate_x_w, od_gate_x_b, od_lru_lambda, od_w_out, ffn_w_in, ffn_w_out, ln_g, ln_b):
    B, S, D = x.shape
    T = B * S
    cos, sin = _rope_tables(positions)
    h = x.reshape(T, D)
    for layer in range(DEPTH):
        j = layer // 2
        h3 = h.reshape(B, S, D)
        if layer % 2 == 0:
            a1, a2 = _even_core(h3, ev_w_in[j].astype(BF16), cos, sin, ev_ret_norm_g[j],
                                ev_pool_w[j].astype(BF16), ev_pool_scale[j])
            w_mix = ev_w_out[j]
        else:
            w_in = od_w_in[j].astype(BF16)
            a1 = _lru_core(h3, w_in[:, :2 * HALF_WIDTH], od_conv_w[j], od_conv_b[j], od_gate_a_w[j], od_gate_a_b[j],
                           od_gate_x_w[j], od_gate_x_b[j], od_lru_lambda[j])
            a2 = _att_rm(*_qkv_rm(h3, w_in[:, 2 * HALF_WIDTH:], cos, sin))
            w_mix = od_w_out[j]
        h = _mix_ffn(h, a1.reshape(T, HALF_WIDTH), a2.reshape(T, HALF_WIDTH), w_mix.astype(BF16),
                     ffn_w_in[layer].astype(BF16), ffn_w_out[layer].astype(BF16), ln_g[layer], ln_b[layer])
    return h.reshape(B, S, D)
```

```python
import functools
import math

import jax
import jax.numpy as jnp
import numpy as np
from jax import lax
from jax.experimental import pallas as pl
from jax.experimental.pallas import tpu as pltpu

F32 = jnp.float32
BF16 = jnp.bfloat16

D_MODEL = 1024
DEPTH = 4
HEAD_DIM = 128
HALF_WIDTH = 512
N_HEADS = HALF_WIDTH // HEAD_DIM
RET_CHUNK = 128
POOL_WINDOWS = (2, 4, 8, 16)
POOL_GROUP = 128
POOL_HALO = 32
LRU_BLOCKS = 8
LRU_C = 8.0
CONV_WIDTH = 4
CONV_HALO = 8
DIL_PATTERNS = ((128, 1), (512, 4), (2048, 16))
ATT_BLOCK = 128
ATT_TILE = 2048
ATT_RES = 16
ROPE_THETA = 10000.0
EVEN_IN = 5 * HALF_WIDTH
ODD_IN = 5 * HALF_WIDTH
D_FF = 2816
DEEPNORM_ALPHA = (2 * DEPTH) ** 0.25
LN_EPS = 1e-5

VMEM_LIMIT = 56 * 1024 * 1024


def _params(*sem):
    return pltpu.CompilerParams(dimension_semantics=sem, vmem_limit_bytes=VMEM_LIMIT)


def _sigmoid(x):
    return 1.0 / (1.0 + jnp.exp(-x))


def _silu(x):
    return x * _sigmoid(x)


def _gelu_tanh(x):
    c = math.sqrt(2.0 / math.pi)
    return x * (0.5 * (1.0 + jnp.tanh(c * (x + 0.044715 * (x * x * x)))))


def _layer_norm(y, g, b):
    mu = jnp.mean(y, axis=-1, keepdims=True)
    d = y - mu
    var = jnp.mean(d * d, axis=-1, keepdims=True)
    return d * lax.rsqrt(var + LN_EPS) * g + b


def _rope(t, cos, sin_signed):
    return t * cos + pltpu.roll(t, HEAD_DIM // 2, axis=1) * sin_signed


def _dot(a, b):
    return jnp.dot(a, b, preferred_element_type=F32)


def _dot_nt(a, b):
    return lax.dot_general(a, b, (((1,), (1,)), ((), ())), preferred_element_type=F32)


def _rope_table_kernel(pos_ref, inv_ref, cos_ref, sin_ref):
    ts = pos_ref.shape[0]
    hs = ts // 2
    half = HEAD_DIM // 2
    lo = lax.broadcasted_iota(jnp.int32, (1, HEAD_DIM), 1) < half
    pos = pos_ref[...].astype(F32)
    ang = jnp.where(lo, pos[0:hs], pos[hs:ts]) * inv_ref[...]
    c = jnp.cos(ang)
    s = jnp.sin(ang)
    cr = pltpu.roll(c, half, axis=1)
    sr = pltpu.roll(s, half, axis=1)
    cos_ref[0:hs, :] = jnp.where(lo, c, cr)
    cos_ref[hs:ts, :] = jnp.where(lo, cr, c)
    sin_ref[0:hs, :] = jnp.where(lo, -s, sr)
    sin_ref[hs:ts, :] = jnp.where(lo, -sr, s)


def _rope_tables(positions):
    B, S = positions.shape
    T = B * S
    ts = 2048
    inv = ROPE_THETA ** (-jnp.arange(0, HEAD_DIM, 2, dtype=F32) / HEAD_DIM)
    inv2 = jnp.concatenate([inv, inv])[None, :]
    row = pl.BlockSpec((1, HEAD_DIM), lambda i: (0, 0))
    cos, sin = pl.pallas_call(
        _rope_table_kernel,
        grid=(T // ts,),
        in_specs=[pl.BlockSpec((ts, 1), lambda i: (i, 0)), row],
        out_specs=[pl.BlockSpec((ts, HEAD_DIM), lambda i: (i, 0))] * 2,
        out_shape=[jax.ShapeDtypeStruct((T, HEAD_DIM), F32)] * 2,
        compiler_params=_params("arbitrary"),
        name="rope_tables",
    )(positions.reshape(T, 1), inv2)
    return cos.reshape(B, S, HEAD_DIM), sin.reshape(B, S, HEAD_DIM)


def _seq_specs(B, nblk, ts):
    last = B * nblk - 1
    cur = lambda w: pl.BlockSpec((None, ts, w), lambda s: (s // nblk, s % nblk, 0))
    first = lambda w: pl.BlockSpec((None, ts, w), lambda s: (0, 0, 0))

    def ahead(w):
        def index_map(s):
            n = jnp.minimum(s + 1, last)
            return (n // nblk, n % nblk, 0)
        return pl.BlockSpec((None, ts, w), index_map)

    return cur, ahead, first


def _project_ahead(step, x0_ref, xn_ref, w_ref, xb_ref, za_ref, zb_ref, body):
    @pl.when(step == 0)
    def _():
        za_ref[...] = _dot(x0_ref[...].astype(BF16), w_ref[...])

    def run(z_cur_ref, z_next_ref):
        def project_piece(k, n):
            if k == 0:
                xb_ref[...] = xn_ref[...].astype(BF16)
            width = w_ref.shape[1] // n
            cols = slice(k * width, (k + 1) * width)
            z_next_ref[:, cols] = _dot(xb_ref[...], w_ref[:, cols])
        body(z_cur_ref, project_piece)

    @pl.when(step % 2 == 0)
    def _():
        run(za_ref, zb_ref)

    @pl.when(step % 2 == 1)
    def _():
        run(zb_ref, za_ref)


def _even_kernel(x0_ref, xn_ref, wi_ref, cos_ref, sin_ref, decay_ref, qdec_ref, kdec_ref, cdec_ref, gn_ref, pw_ref,
                 ps_ref, ret_ref, pool_ref, xb_ref, za_ref, zb_ref, state_ref, pbuf_ref, s2_ref, s4_ref, s8_ref,
                 *, ts, nblk):
    step = pl.program_id(0)
    sblk = step % nblk
    H = POOL_HALO

    @pl.when(sblk == 0)
    def _():
        state_ref[...] = jnp.zeros_like(state_ref)
        pbuf_ref[0:H, :] = jnp.zeros((H, HALF_WIDTH), F32)

    body = functools.partial(_even_body, cos_ref, sin_ref, decay_ref, qdec_ref, kdec_ref, cdec_ref, gn_ref, pw_ref,
                             ps_ref, ret_ref, pool_ref, state_ref, pbuf_ref, s2_ref, s4_ref, s8_ref, sblk, ts)
    _project_ahead(step, x0_ref, xn_ref, wi_ref, xb_ref, za_ref, zb_ref, body)


def _block_diag2(a, b):
    za = jnp.zeros_like(a)
    zb = jnp.zeros_like(b)
    return jnp.concatenate([jnp.concatenate([a, zb], axis=1), jnp.concatenate([za, b], axis=1)], axis=0)


def _pair(ref, hs):
    return jnp.concatenate([ref[h] for h in hs], axis=1)


def _even_body(cos_ref, sin_ref, decay_ref, qdec_ref, kdec_ref, cdec_ref, gn_ref, pw_ref, ps_ref, ret_ref, pool_ref,
               state_ref, pbuf_ref, s2_ref, s4_ref, s8_ref, sblk, ts, z_ref, project_piece):
    H = POOL_HALO
    npieces = ts // RET_CHUNK + 1
    for c in range(ts // RET_CHUNK):
        project_piece(c, npieces)
        rows = pl.ds(c * RET_CHUNK, RET_CHUNK)
        cos = cos_ref[rows, :]
        sin = sin_ref[rows, :]
        for h0 in range(0, N_HEADS, 2):
            hs = (h0, h0 + 1)
            q = [_rope(z_ref[rows, pl.ds(h * HEAD_DIM, HEAD_DIM)], cos, sin) for h in hs]
            k = [_rope(z_ref[rows, pl.ds(HALF_WIDTH + h * HEAD_DIM, HEAD_DIM)], cos, sin) * (HEAD_DIM ** -0.5)
                 for h in hs]
            vb = [z_ref[rows, pl.ds(2 * HALF_WIDTH + h * HEAD_DIM, HEAD_DIM)].astype(BF16) for h in hs]
            state = [state_ref[h] for h in hs]
            qb = jnp.concatenate([t.astype(BF16) for t in q], axis=1)
            scores = _dot_nt(qb, _block_diag2(*[t.astype(BF16) for t in k])) * _pair(decay_ref, hs)
            vbd = _block_diag2(*vb)
            inner = _dot(scores.astype(BF16), vbd)
            cross = _dot(qb, _block_diag2(*[s.astype(BF16) for s in state])) * _pair(qdec_ref, hs)
            kd = jnp.concatenate([(k[i] * kdec_ref[h]).T.astype(BF16) for i, h in enumerate(hs)], axis=1)
            kv = _dot(kd, vbd)
            y2 = inner + cross
            for i, h in enumerate(hs):
                lanes = slice(h * HEAD_DIM, (h + 1) * HEAD_DIM)
                half = slice(i * HEAD_DIM, (i + 1) * HEAD_DIM)
                state_ref[h] = state[i] * cdec_ref[h] + kv[:, half]
                y = y2[:, half]
                gate = z_ref[rows, pl.ds(3 * HALF_WIDTH + h * HEAD_DIM, HEAD_DIM)]
                mu = jnp.mean(y, axis=-1, keepdims=True)
                d = y - mu
                var = jnp.mean(d * d, axis=-1, keepdims=True)
                yn = d * lax.rsqrt(var + LN_EPS) * gn_ref[:, lanes]
                ret_ref[rows, lanes] = (yn * _silu(gate)).astype(ret_ref.dtype)

    project_piece(npieces - 1, npieces)
    pbuf_ref[H:H + ts, :] = z_ref[:, pl.ds(4 * HALF_WIDTH, HALF_WIDTH)]
    G = POOL_GROUP
    s2_ref[8:H + ts, :] = pbuf_ref[8:H + ts, :] + pbuf_ref[7:H + ts - 1, :]
    s4_ref[16:H + ts, :] = s2_ref[16:H + ts, G:4 * G] + s2_ref[14:H + ts - 2, G:4 * G]
    s8_ref[24:H + ts, :] = s4_ref[24:H + ts, G:3 * G] + s4_ref[20:H + ts - 4, G:3 * G]
    s16 = s8_ref[32:H + ts, G:2 * G] + s8_ref[24:H + ts - 8, G:2 * G]
    wsums = (s2_ref[H:H + ts, 0:G], s4_ref[H:H + ts, 0:G], s8_ref[H:H + ts, 0:G], s16)
    tpos = sblk * ts + lax.broadcasted_iota(jnp.int32, (ts, 1), 0) + 1
    for gi, w in enumerate(POOL_WINDOWS):
        lanes = slice(gi * G, (gi + 1) * G)
        cnt = jnp.minimum(tpos, w).astype(F32)
        pooled = wsums[gi] / cnt - pbuf_ref[H:H + ts, lanes]
        mixed = _dot(pooled.astype(BF16), pw_ref[gi]) * ps_ref[:, lanes]
        pool_ref[:, lanes] = mixed.astype(pool_ref.dtype)
    pbuf_ref[0:H, :] = pbuf_ref[ts:ts + H, :]


def _even_core(x3, w_in_bf16, j, cos, sin, ret_norm_g, pool_w_bf16, pool_scale):
    B, S, D = x3.shape
    ts = 512
    C = RET_CHUNK
    lg = jnp.log1p(-(2.0 ** (-5.0 - jnp.arange(N_HEADS, dtype=F32))))
    idx = jnp.arange(C, dtype=F32)
    rel = idx[:, None] - idx[None, :]
    decay = jnp.where(rel[None] >= 0, jnp.exp(jnp.maximum(rel, 0.0)[None] * lg[:, None, None]), 0.0)
    k_decay = jnp.exp((C - 1 - idx)[None, :] * lg[:, None])
    q_decay = jnp.exp((idx + 1.0)[None, :] * lg[:, None])
    chunk_decay = jnp.exp(C * lg)
    qdec = jnp.broadcast_to(q_decay[:, :, None], (N_HEADS, C, HEAD_DIM))
    kdec = jnp.broadcast_to(k_decay[:, :, None], (N_HEADS, C, HEAD_DIM))
    cdec = jnp.broadcast_to(chunk_decay[:, None, None], (N_HEADS, 1, HEAD_DIM))

    const3 = lambda shape: pl.BlockSpec(shape, lambda s: (0, 0, 0), pipeline_mode=pl.Buffered(1))
    const2 = lambda shape: pl.BlockSpec(shape, lambda s: (0, 0), pipeline_mode=pl.Buffered(1))
    nblk = S // ts
    seq, ahead, first = _seq_specs(B, nblk, ts)
    return pl.pallas_call(
        functools.partial(_even_kernel, ts=ts, nblk=nblk),
        grid=(B * nblk,),
        in_specs=[first(D), ahead(D),
                  pl.BlockSpec((None, D, EVEN_IN), lambda s: (j, 0, 0), pipeline_mode=pl.Buffered(1)),
                  seq(HEAD_DIM), seq(HEAD_DIM),
                  const3((N_HEADS, C, C)), const3((N_HEADS, C, HEAD_DIM)), const3((N_HEADS, C, HEAD_DIM)),
                  const3((N_HEADS, 1, HEAD_DIM)), const2((1, HALF_WIDTH)),
                  const3((len(POOL_WINDOWS), POOL_GROUP, POOL_GROUP)), const2((1, HALF_WIDTH))],
        out_specs=[seq(HALF_WIDTH), seq(HALF_WIDTH)],
        out_shape=[jax.ShapeDtypeStruct((B, S, HALF_WIDTH), BF16)] * 2,
        scratch_shapes=[pltpu.VMEM((ts, D), BF16), pltpu.VMEM((ts, EVEN_IN), F32), pltpu.VMEM((ts, EVEN_IN), F32),
                        pltpu.VMEM((N_HEADS, HEAD_DIM, HEAD_DIM), F32),
                        pltpu.VMEM((POOL_HALO + ts, HALF_WIDTH), F32),
                        pltpu.VMEM((POOL_HALO + ts, HALF_WIDTH), F32),
                        pltpu.VMEM((POOL_HALO + ts, 3 * POOL_GROUP), F32),
                        pltpu.VMEM((POOL_HALO + ts, 2 * POOL_GROUP), F32)],
        compiler_params=_params("arbitrary"),
        name="even_core",
    )(x3, x3, w_in_bf16, cos, sin, decay, qdec, kdec, cdec, ret_norm_g[None, :], pool_w_bf16, pool_scale[None, :])


def _lru_kernel(x_ref, wi_ref, cw_ref, cb_ref, wa_ref, ba_ref, wx_ref, bx_ref, lam_ref, y_ref,
                zg_ref, ubuf_ref, hcar_ref, h_ref, *, ts, nblk):
    step = pl.program_id(0)
    H = CONV_HALO

    @pl.when(step % nblk == 0)
    def _():
        ubuf_ref[0:H, :] = jnp.zeros((H, HALF_WIDTH), F32)
        hcar_ref[...] = jnp.zeros_like(hcar_ref)

    xb = x_ref[...].astype(BF16)
    ubuf_ref[H:H + ts, :] = _dot(xb, wi_ref[:, HALF_WIDTH:2 * HALF_WIDTH])
    zg_ref[...] = _dot(xb, wi_ref[:, 0:HALF_WIDTH])
    u = cb_ref[...] + cw_ref[CONV_WIDTH - 1:CONV_WIDTH, :] * ubuf_ref[H:H + ts, :]
    for k in range(CONV_WIDTH - 1):
        off = H - (CONV_WIDTH - 1) + k
        u = u + cw_ref[k:k + 1, :] * ubuf_ref[off:off + ts, :]
    ubuf_ref[0:H, :] = ubuf_ref[ts:ts + H, :]

    ub = u.astype(BF16)
    r = _sigmoid(_dot(ub, wa_ref[...]) + ba_ref[...])
    ig = _sigmoid(_dot(ub, wx_ref[...]) + bx_ref[...])
    nl = -lam_ref[...]
    softplus = jnp.maximum(nl, 0.0) + jnp.log1p(jnp.exp(-jnp.abs(nl)))
    log_a = (-LRU_C * r) * softplus
    a = jnp.exp(log_a)
    bseq = jnp.sqrt(jnp.tanh(-log_a) * (1.0 + a * a)) * (ig * u)

    G = 8
    a = a.reshape(ts // G, G, HALF_WIDTH)
    bseq = bseq.reshape(ts // G, G, HALF_WIDTH)
    row = lax.broadcasted_iota(jnp.int32, (1, G, 1), 1)
    sh = 1
    while sh < G:
        a_prev = pltpu.roll(a, sh, axis=1)
        b_prev = pltpu.roll(bseq, sh, axis=1)
        m = row >= sh
        bseq = jnp.where(m, a * b_prev + bseq, bseq)
        a = jnp.where(m, a * a_prev, a)
        sh *= 2
    carry = hcar_ref[0:1, :]
    for v in range(ts // G):
        hv = bseq[v] + a[v] * carry
        h_ref[G * v:G * (v + 1), :] = hv
        carry = hv[G - 1:G]
    hcar_ref[...] = jnp.broadcast_to(carry, hcar_ref.shape)
    y_ref[...] = (h_ref[...] * _gelu_tanh(zg_ref[...])).astype(y_ref.dtype)


def _block_diag(w):
    n, c, d = w.shape
    eye = jnp.eye(n, dtype=w.dtype)
    return (w[:, :, None, :] * eye[:, None, :, None]).reshape(n * c, n * d)


def _lru_core(x3, w_in_bf16, j, conv_w, conv_b, gate_a_w, gate_a_b, gate_x_w, gate_x_b, lam):
    B, S, D = x3.shape
    ts = 512
    W = HALF_WIDTH
    wa = _block_diag(gate_a_w).astype(BF16)
    wx = _block_diag(gate_x_w).astype(BF16)
    const = lambda shape: pl.BlockSpec(shape, lambda s: (0, 0), pipeline_mode=pl.Buffered(1))
    nblk = S // ts
    seq, _, _ = _seq_specs(B, nblk, ts)
    return pl.pallas_call(
        functools.partial(_lru_kernel, ts=ts, nblk=nblk),
        grid=(B * nblk,),
        in_specs=[seq(D), pl.BlockSpec((None, D, 2 * W), lambda s: (j, 0, 0), pipeline_mode=pl.Buffered(1)),
                  const((CONV_WIDTH, W)), const((1, W)), const((W, W)), const((1, W)),
                  const((W, W)), const((1, W)), const((1, W))],
        out_specs=seq(W),
        out_shape=jax.ShapeDtypeStruct((B, S, W), BF16),
        scratch_shapes=[pltpu.VMEM((ts, W), F32),
                        pltpu.VMEM((CONV_HALO + ts, W), F32), pltpu.VMEM((8, W), F32), pltpu.VMEM((ts, W), F32)],
        compiler_params=_params("arbitrary"),
        name="lru_core",
    )(x3, w_in_bf16, conv_w, conv_b[None, :], wa, gate_a_b[None, :], wx, gate_x_b[None, :], lam[None, :])


def _qkv_rm_kernel(x_ref, w_ref, cos_ref, sin_ref, q_ref, k_ref, v_ref, xs_ref):
    R = ATT_RES
    rows_per = x_ref.shape[0] // R
    ncol = x_ref.shape[1] // HEAD_DIM

    def residue_rows(ref, r):
        return ref[pl.ds(r, rows_per, stride=R), :]

    for c in range(ncol):
        xs_ref[c] = x_ref[:, c * HEAD_DIM:(c + 1) * HEAD_DIM]
    qscale = HEAD_DIM ** -0.5 * math.log2(math.e)
    half = R // 2
    for r0 in (0, half):
        x = jnp.concatenate(
            [jnp.concatenate([residue_rows(xs_ref.at[c], r) for c in range(ncol)], axis=1)
             for r in range(r0, r0 + half)], axis=0).astype(BF16)
        z = _dot(x, w_ref[...])
        for r in range(r0, r0 + half):
            rows = slice((r - r0) * rows_per, (r - r0 + 1) * rows_per)
            cos = residue_rows(cos_ref, r)
            sin = residue_rows(sin_ref, r)
            for h in range(N_HEADS):
                lanes = slice(h * HEAD_DIM, (h + 1) * HEAD_DIM)
                q_ref[r, :, lanes] = _rope(z[rows, h * HEAD_DIM:(h + 1) * HEAD_DIM], cos, sin) * qscale
                k_ref[r, :, lanes] = _rope(z[rows, HALF_WIDTH + h * HEAD_DIM:HALF_WIDTH + (h + 1) * HEAD_DIM],
                                           cos, sin)
            v_ref[r] = z[rows, 2 * HALF_WIDTH:3 * HALF_WIDTH]


def _qkv_rm(x3, w_bf16, cos, sin):
    B, S, D = x3.shape
    R = ATT_RES
    I = S // R
    W = HALF_WIDTH
    ts = 1024
    seq = lambda w: pl.BlockSpec((None, ts, w), lambda b, t: (b, t, 0))
    out_spec = pl.BlockSpec((None, R, ts // R, W), lambda b, t: (b, 0, t, 0))
    return pl.pallas_call(
        _qkv_rm_kernel,
        grid=(B, S // ts),
        in_specs=[seq(D), pl.BlockSpec((D, 3 * W), lambda b, t: (0, 0), pipeline_mode=pl.Buffered(1)),
                  seq(HEAD_DIM), seq(HEAD_DIM)],
        out_specs=[out_spec] * 3,
        out_shape=[jax.ShapeDtypeStruct((B, R, I, W), F32)] * 3,
        scratch_shapes=[pltpu.VMEM((D // HEAD_DIM, ts, HEAD_DIM), F32)],
        compiler_params=_params("arbitrary", "arbitrary"),
        name="qkv_rm",
    )(x3, w_bf16, cos, sin)


def _pattern_bias(dil):
    nslab = ATT_RES // dil
    slab = ATT_BLOCK // nslab
    row = np.arange(ATT_BLOCK)
    col = np.arange(2 * ATT_BLOCK)
    jq = nslab * (row % slab) + row // slab + ATT_BLOCK
    jk = nslab * (col % slab) + (col % ATT_BLOCK) // slab + ATT_BLOCK * (col // ATT_BLOCK)
    delta = jq[:, None] - jk[None, :]
    return np.where((delta >= 0) & (delta <= ATT_BLOCK), 0.0, -np.inf).astype(np.float32)


def _att_rm_kernel(q_ref, k_ref, v_ref, bias_ref, o_ref, num_ref, den_ref, mx_ref, onat_ref):
    QB = ATT_BLOCK
    TI = ATT_TILE // ATT_RES
    ones = jnp.ones((2 * QB, HEAD_DIM), BF16)

    def gather(ref, slabs, r0, rn):
        return jnp.concatenate([ref[r, r0:r0 + rn, :] for r in slabs], axis=0)

    for tile in range(q_ref.shape[1] // TI):
        i0 = tile * TI
        for p, (window, dil) in enumerate(DIL_PATTERNS):
            assert window // dil == QB
            nslab = ATT_RES // dil
            slab = QB // nslab
            for n in range(ATT_TILE // QB):
                res, m = n % dil, n // dil
                slabs = [res + dil * a for a in range(nslab)]
                c0 = i0 + slab * m
                qb = gather(q_ref, slabs, c0, slab).astype(BF16)
                if c0 == 0:
                    kb = gather(k_ref, slabs, c0, slab).astype(BF16)
                    vb = gather(v_ref, slabs, c0, slab).astype(BF16)
                    bias = bias_ref[p, :, QB:2 * QB]
                else:
                    kb = jnp.concatenate([gather(k_ref, slabs, c0 - slab, slab),
                                          gather(k_ref, slabs, c0, slab)], axis=0).astype(BF16)
                    vb = jnp.concatenate([gather(v_ref, slabs, c0 - slab, slab),
                                          gather(v_ref, slabs, c0, slab)], axis=0).astype(BF16)
                    bias = bias_ref[p]
                s = _dot_nt(qb, kb) + bias
                mx = jnp.max(s, axis=-1, keepdims=True)
                e = jnp.exp2(s - mx).astype(BF16)
                nd = _dot(e, jnp.concatenate([vb, ones[0:vb.shape[0]]], axis=1))
                mxb = jnp.broadcast_to(mx, (QB, HEAD_DIM))
                for a, r in enumerate(slabs):
                    rows = slice(a * slab, (a + 1) * slab)
                    dst = slice(slab * m, slab * (m + 1))
                    num_ref[p, r, dst, :] = nd[rows, 0:HEAD_DIM]
                    den_ref[p, r, dst, :] = nd[rows, HEAD_DIM:2 * HEAD_DIM]
                    mx_ref[p, r, dst, :] = mxb[rows]

        m0, m1, m2 = mx_ref[0], mx_ref[1], mx_ref[2]
        mm = jnp.maximum(jnp.maximum(m0, m1), m2)
        w0, w1, w2 = jnp.exp2(m0 - mm), jnp.exp2(m1 - mm), jnp.exp2(m2 - mm)
        num = w0 * num_ref[0] + w1 * num_ref[1] + w2 * num_ref[2]
        den = w0 * den_ref[0] + w1 * den_ref[1] + w2 * den_ref[2]
        out = num / den
        for r in range(ATT_RES):
            onat_ref[pl.ds(r, TI, stride=ATT_RES), :] = out[r]
        o_ref[tile * ATT_TILE:(tile + 1) * ATT_TILE, :] = onat_ref[...].astype(o_ref.dtype)


def _att_rm(q, k, v):
    B, R, I, W = q.shape
    TI = ATT_TILE // ATT_RES
    assert R == ATT_RES and I % TI == 0
    bias = jnp.asarray(np.stack([_pattern_bias(dil) for _, dil in DIL_PATTERNS]))
    blk = pl.BlockSpec((None, R, I, HEAD_DIM), lambda b, h: (b, 0, 0, h))
    acc = pltpu.VMEM((len(DIL_PATTERNS), R, TI, HEAD_DIM), F32)
    return pl.pallas_call(
        _att_rm_kernel,
        grid=(B, N_HEADS),
        in_specs=[blk, blk, blk,
                  pl.BlockSpec(bias.shape, lambda b, h: (0, 0, 0), pipeline_mode=pl.Buffered(1))],
        out_specs=pl.BlockSpec((None, R * I, HEAD_DIM), lambda b, h: (b, 0, h)),
        out_shape=jax.ShapeDtypeStruct((B, R * I, W), BF16),
        scratch_shapes=[acc, acc, acc, pltpu.VMEM((ATT_TILE, HEAD_DIM), F32)],
        compiler_params=_params("arbitrary", "arbitrary"),
        name="att_rm",
    )(q, k, v, bias)


FFN_SUB = 512
FFN_CHUNK = 256


def _mix_ffn_kernel(x_ref, a1_ref, a2_ref, wm_ref, wi_ref, wo_ref, g_ref, b_ref, o_ref, *, tm):
    W = HALF_WIDTH
    nsub = tm // FFN_SUB
    h1 = [None] * nsub
    hb = [None] * nsub

    def prologue(i):
        rows = slice(i * FFN_SUB, (i + 1) * FFN_SUB)
        mix = _dot(a1_ref[rows, :], wm_ref[0:W, :]) + _dot(a2_ref[rows, :], wm_ref[W:2 * W, :])
        h1[i] = _layer_norm(DEEPNORM_ALPHA * x_ref[rows, :] + mix, g_ref[0:1, :], b_ref[0:1, :])
        hb[i] = h1[i].astype(BF16)

    def epilogue(i, acc):
        rows = slice(i * FFN_SUB, (i + 1) * FFN_SUB)
        o_ref[rows, :] = _layer_norm(DEEPNORM_ALPHA * h1[i] + acc, g_ref[1:2, :], b_ref[1:2, :])

    prologue(0)
    prev_acc = None
    for i in range(nsub):
        acc = None
        for ci, c0 in enumerate(range(0, D_FF, FFN_CHUNK)):
            if ci == 1 and i + 1 < nsub:
                prologue(i + 1)
            if ci == 1 and i > 0:
                epilogue(i - 1, prev_acc)
            gate = _dot(hb[i], wi_ref[:, c0:c0 + FFN_CHUNK])
            up = _dot(hb[i], wi_ref[:, D_FF + c0:D_FF + c0 + FFN_CHUNK])
            part = _dot((_silu(gate) * up).astype(BF16), wo_ref[c0:c0 + FFN_CHUNK, :])
            acc = part if acc is None else acc + part
        prev_acc = acc
    epilogue(nsub - 1, prev_acc)


def _mix_ffn(x2d, a1, a2, w_mix_bf16, j, w_in_bf16, w_out_bf16, layer, g_all, b_all):
    T, D = x2d.shape
    tm = 1024
    rowblk = lambda w: pl.BlockSpec((tm, w), lambda i: (i, 0))
    resident = lambda shape, idx: pl.BlockSpec((None,) + shape, lambda i: (idx, 0, 0), pipeline_mode=pl.Buffered(1))
    return pl.pallas_call(
        functools.partial(_mix_ffn_kernel, tm=tm),
        grid=(T // tm,),
        in_specs=[rowblk(D), rowblk(HALF_WIDTH), rowblk(HALF_WIDTH), resident((D, D), j),
                  resident((D, 2 * D_FF), layer), resident((D_FF, D), layer),
                  resident((2, D), layer), resident((2, D), layer)],
        out_specs=rowblk(D),
        out_shape=jax.ShapeDtypeStruct((T, D), F32),
        compiler_params=_params("arbitrary"),
        name="mix_ffn",
    )(x2d, a1, a2, w_mix_bf16, w_in_bf16, w_out_bf16, g_all, b_all)


def kernel(x, positions, ev_w_in, ev_ret_norm_g, ev_pool_w, ev_pool_scale, ev_w_out, od_w_in, od_conv_w, od_conv_b, od_gate_a_w, od_gate_a_b, od_gate_x_w, od_gate_x_b, od_lru_lambda, od_w_out, ffn_w_in, ffn_w_out, ln_g, ln_b):
    B, S, D = x.shape
    T = B * S
    cos, sin = _rope_tables(positions)
    h = x.reshape(T, D)
    ev_wi, ev_wo = ev_w_in.astype(BF16), ev_w_out.astype(BF16)
    od_wi, od_wo = od_w_in.astype(BF16), od_w_out.astype(BF16)
    ffn_wi, ffn_wo = ffn_w_in.astype(BF16), ffn_w_out.astype(BF16)
    for layer in range(DEPTH):
        j = layer // 2
        h3 = h.reshape(B, S, D)
        if layer % 2 == 0:
            a1, a2 = _even_core(h3, ev_wi, j, cos, sin, ev_ret_norm_g[j], ev_pool_w[j].astype(BF16), ev_pool_scale[j])
            w_mix = ev_wo
        else:
            a1 = _lru_core(h3, od_wi, j, od_conv_w[j], od_conv_b[j], od_gate_a_w[j], od_gate_a_b[j],
                           od_gate_x_w[j], od_gate_x_b[j], od_lru_lambda[j])
            a2 = _att_rm(*_qkv_rm(h3, od_wi[j][:, 2 * HALF_WIDTH:], cos, sin))
            w_mix = od_wo
        h = _mix_ffn(h, a1.reshape(T, HALF_WIDTH), a2.reshape(T, HALF_WIDTH), w_mix, j,
                     ffn_wi, ffn_wo, layer, ln_g, ln_b)
    return h.reshape(B, S, D)
```
